```python
import math
import jax, jax.numpy as jnp
from jax import lax
import numpy as np

D_MODEL = 1024
BATCH = 4
SEQ = 4096
DEPTH = 4
DEC_BATCH = 128
DEC_SEQ = 1
PAST_LEN = 2048
PAGE_SIZE = 128

LRU_W = D_MODEL // 2
LRU_BLOCKS = 8
LRU_BS = LRU_W // LRU_BLOCKS
CONV_W = 4
C_GATE = 8.0
ATT_W = D_MODEL // 2
QK_HD = 64
V_HD = 2 * QK_HD
N_HEADS = ATT_W // V_HD
ROT_DIM = QK_HD // 4
ROPE_THETA = 500000.0
Q_BLOCK = 128
D_FF = 2816
P_DIM = 256
EPS = 1e-6
IN_W = 2 * LRU_W + 2 * N_HEADS * 2 * QK_HD + N_HEADS * V_HD
NEG = -1e30

kernel_name = 'hymba_rglru_diffattn_macaron_step'


def rmsnorm(x, g):
    xf = x.astype(jnp.float32)
    y = xf * lax.rsqrt(jnp.mean(xf * xf, axis=-1, keepdims=True) + EPS)
    return (y * g.astype(jnp.float32)).astype(x.dtype)


def swiglu(h, wg, wu, wd):
    return (jax.nn.silu(h @ wg) * (h @ wu)) @ wd


def rope(x, pos):
    half = ROT_DIM // 2
    freqs = jnp.power(jnp.float32(ROPE_THETA), -jnp.arange(0, ROT_DIM, 2, dtype=jnp.float32) / ROT_DIM)
    ang = pos.astype(jnp.float32)[:, None] * freqs[None, :]
    cos = jnp.cos(ang)[None, :, None, None, :]
    sin = jnp.sin(ang)[None, :, None, None, :]
    xr = x[..., :ROT_DIM].astype(jnp.float32)
    x1, x2 = xr[..., :half], xr[..., half:]
    rot = jnp.concatenate([x1 * cos - x2 * sin, x1 * sin + x2 * cos], axis=-1).astype(x.dtype)
    return jnp.concatenate([rot, x[..., ROT_DIM:]], axis=-1)


def diff_attend(q, k, v, q_pos, k_pos, lam):
    s = jnp.einsum('bqhcd,bkhcd->bhcqk', q, k).astype(jnp.float32) * (QK_HD ** -0.5)
    mask = k_pos[None, :] <= q_pos[:, None]
    s = jnp.where(mask, s, jnp.float32(NEG))
    a = jax.nn.softmax(s, axis=-1)
    w = a[:, :, 0] - lam * a[:, :, 1]
    return jnp.einsum('bhqk,bkhe->bqhe', w.astype(v.dtype), v)


def rglru(u, conv_buf, h0, conv_w, conv_b, w_a, b_a, w_i, b_i, a_param):
    B, T, C = u.shape
    up = jnp.concatenate([conv_buf.astype(u.dtype), u], axis=1)
    xc = conv_b.astype(u.dtype)
    for j in range(CONV_W):
        xc = xc + up[:, j:j + T] * conv_w[j]
    new_buf = up[:, -(CONV_W - 1):]
    xb = xc.reshape(B, T, LRU_BLOCKS, LRU_BS)
    r = jax.nn.sigmoid(jnp.einsum('btnc,ncd->btnd', xb, w_a).reshape(B, T, C) + b_a)
    i = jax.nn.sigmoid(jnp.einsum('btnc,ncd->btnd', xb, w_i).reshape(B, T, C) + b_i)
    log_a = -C_GATE * r.astype(jnp.float32) * jax.nn.softplus(-a_param.astype(jnp.float32))
    a = jnp.exp(log_a)
    b = jnp.sqrt(-jnp.expm1(2.0 * log_a)) * (i * xc).astype(jnp.float32)
    b = b.at[:, 0].add(a[:, 0] * h0.astype(jnp.float32))

    def combine(e1, e2):
        a1, b1 = e1
        a2, b2 = e2
        return a1 * a2, a2 * b1 + b2

    _, h = lax.associative_scan(combine, (a, b), axis=1)
    return h, new_buf, h[:, -1]


def trunk(x, p, pos, past, conv0, h0, w_in, w_out, conv_w, conv_b, w_a, b_a, w_i, b_i,
          a_param, w_lambda, g_subln, w_ffn_gate, w_ffn_up, w_ffn_down, w_pe, w_pg,
          g_norm, g_final):
    B, T, _ = x.shape
    ks, vs, convs, hs = [], [], [], []
    splits = [LRU_W, 2 * LRU_W, 2 * LRU_W + N_HEADS * 2 * QK_HD, 2 * LRU_W + 2 * N_HEADS * 2 * QK_HD]
    for l in range(DEPTH):
        x = x + 0.5 * swiglu(rmsnorm(x, g_norm[l, 0]), w_ffn_gate[l, 0], w_ffn_up[l, 0], w_ffn_down[l, 0])
        h = rmsnorm(x, g_norm[l, 1])
        proj = h @ w_in[l]
        u, gate, q, k, v = jnp.split(proj, splits, axis=-1)
        q = rope(q.reshape(B, T, N_HEADS, 2, QK_HD), pos)
        k = rope(k.reshape(B, T, N_HEADS, 2, QK_HD), pos)
        v = v.reshape(B, T, N_HEADS, V_HD)
        lam_init = 0.8 - 0.6 * math.exp(-0.3 * l)
        wl = w_lambda[l].astype(jnp.float32)
        lam = jnp.exp(jnp.sum(wl[0] * wl[1])) - jnp.exp(jnp.sum(wl[2] * wl[3])) + lam_init
        if past is None:
            nb = T // Q_BLOCK
            qb = q.reshape(B, nb, Q_BLOCK, N_HEADS, 2, QK_HD).swapaxes(0, 1)
            pb = pos.reshape(nb, Q_BLOCK)
            ob = lax.map(lambda args: diff_attend(args[0], k, v, args[1], pos, lam), (qb, pb))
            o = ob.swapaxes(0, 1).reshape(B, T, N_HEADS, V_HD)
        else:
            cache_k, cache_v, page_table = past
            ck = cache_k[l][page_table].reshape(B, -1, N_HEADS, 2, QK_HD).astype(k.dtype)
            cv = cache_v[l][page_table].reshape(B, -1, N_HEADS, V_HD).astype(v.dtype)
            k_all = jnp.concatenate([ck, k], axis=1)
            v_all = jnp.concatenate([cv, v], axis=1)
            k_pos = jnp.arange(ck.shape[1] + T)
            o = diff_attend(q, k_all, v_all, pos, k_pos, lam)
        o_attn = (rmsnorm(o, g_subln[l]) * (1.0 - lam_init)).reshape(B, T, ATT_W)
        hl, new_buf, h_last = rglru(u, conv0[l], h0[l], conv_w[l], conv_b[l], w_a[l], b_a[l],
                                    w_i[l], b_i[l], a_param[l])
        o_lru = hl.astype(x.dtype) * jax.nn.gelu(gate)
        x = x + jnp.concatenate([o_lru, o_attn], axis=-1) @ w_out[l]
        x = x + 0.5 * swiglu(rmsnorm(x, g_norm[l, 2]), w_ffn_gate[l, 1], w_ffn_up[l, 1], w_ffn_down[l, 1])
        x = x + jax.nn.sigmoid(rmsnorm(x, g_norm[l, 3]) @ w_pg[l]) * (p[l] @ w_pe[l])
        ks.append(k.reshape(B, T, N_HEADS, 2 * QK_HD))
        vs.append(v)
        convs.append(new_buf)
        hs.append(h_last)
    y = rmsnorm(x, g_final)
    return y, jnp.stack(ks), jnp.stack(vs), jnp.stack(convs), jnp.stack(hs)


def setup_inputs(seed: int = 0) -> dict:
    key = jax.random.key(seed)
    ks = jax.random.split(key, 40)
    f32 = jnp.float32
    n_pages = PAST_LEN // PAGE_SIZE
    n_used = DEC_BATCH * n_pages
    n_pool = n_used + n_used // 4
    nrm = lambda k, s, sc: jax.random.normal(k, s, f32) * sc
    page_table = jax.random.permutation(ks[0], n_pool)[:n_used].reshape(DEC_BATCH, n_pages).astype(jnp.int32)
    u = jax.random.uniform(ks[1], (DEPTH, LRU_W), f32, 0.9, 0.999)
    a_base = u ** (1.0 / C_GATE)
    a_param = jnp.log(a_base) - jnp.log1p(-a_base)
    return {
        'x_prompt': nrm(ks[2], (BATCH, SEQ, D_MODEL), 1.0),
        'x_sample': nrm(ks[3], (DEC_BATCH, DEC_SEQ, D_MODEL), 1.0),
        'cache_k': nrm(ks[4], (DEPTH, n_pool, PAGE_SIZE, N_HEADS, 2 * QK_HD), 1.0),
        'cache_v': nrm(ks[5], (DEPTH, n_pool, PAGE_SIZE, N_HEADS, V_HD), 1.0),
        'page_table': page_table,
        'state_conv': nrm(ks[6], (DEPTH, DEC_BATCH, CONV_W - 1, LRU_W), 1.0),
        'state_h': nrm(ks[7], (DEPTH, DEC_BATCH, LRU_W), 0.5),
        'p_prompt': nrm(ks[8], (DEPTH, BATCH, SEQ, P_DIM), 1.0),
        'p_sample': nrm(ks[9], (DEPTH, DEC_BATCH, DEC_SEQ, P_DIM), 1.0),
        'w_in': nrm(ks[10], (DEPTH, D_MODEL, IN_W), D_MODEL ** -0.5),
        'w_out': nrm(ks[11], (DEPTH, D_MODEL, D_MODEL), D_MODEL ** -0.5),
        'conv_w': nrm(ks[12], (DEPTH, CONV_W, LRU_W), CONV_W ** -0.5),
        'conv_b': nrm(ks[13], (DEPTH, LRU_W), 0.01),
        'w_a': nrm(ks[14], (DEPTH, LRU_BLOCKS, LRU_BS, LRU_BS), LRU_BS ** -0.5),
        'b_a': nrm(ks[15], (DEPTH, LRU_W), 0.01),
        'w_i': nrm(ks[16], (DEPTH, LRU_BLOCKS, LRU_BS, LRU_BS), LRU_BS ** -0.5),
        'b_i': nrm(ks[17], (DEPTH, LRU_W), 0.01),
        'a_param': a_param,
        'w_lambda': nrm(ks[18], (DEPTH, 4, QK_HD), 0.1),
        'g_subln': 1.0 + nrm(ks[19], (DEPTH, V_HD), 0.02),
        'w_ffn_gate': nrm(ks[20], (DEPTH, 2, D_MODEL, D_FF), D_MODEL ** -0.5),
        'w_ffn_up': nrm(ks[21], (DEPTH, 2, D_MODEL, D_FF), D_MODEL ** -0.5),
        'w_ffn_down': nrm(ks[22], (DEPTH, 2, D_FF, D_MODEL), D_FF ** -0.5),
        'w_pe': nrm(ks[23], (DEPTH, P_DIM, D_MODEL), P_DIM ** -0.5),
        'w_pg': nrm(ks[24], (DEPTH, D_MODEL, D_MODEL), D_MODEL ** -0.5),
        'g_norm': 1.0 + nrm(ks[25], (DEPTH, 4, D_MODEL), 0.02),
        'g_final': 1.0 + nrm(ks[26], (D_MODEL,), 0.02),
    }


def reference(x_prompt, x_sample, cache_k, cache_v, page_table, state_conv, state_h,
              p_prompt, p_sample, w_in, w_out, conv_w, conv_b, w_a, b_a, w_i, b_i, a_param,
              w_lambda, g_subln, w_ffn_gate, w_ffn_up, w_ffn_down, w_pe, w_pg, g_norm, g_final):
    weights = (w_in, w_out, conv_w, conv_b, w_a, b_a, w_i, b_i, a_param, w_lambda, g_subln,
               w_ffn_gate, w_ffn_up, w_ffn_down, w_pe, w_pg, g_norm, g_final)
    B, T, _ = x_prompt.shape
    conv0 = jnp.zeros((DEPTH, B, CONV_W - 1, LRU_W), x_prompt.dtype)
    h0 = jnp.zeros((DEPTH, B, LRU_W), jnp.float32)
    pos_p = jnp.arange(T)
    y_prompt, k_prompt, v_prompt, conv_prompt, h_prompt = trunk(
        x_prompt, p_prompt, pos_p, None, conv0, h0, *weights)
    Ts = x_sample.shape[1]
    pos_s = PAST_LEN + jnp.arange(Ts)
    y_sample, k_sample, v_sample, conv_sample, h_sample = trunk(
        x_sample, p_sample, pos_s, (cache_k, cache_v, page_table), state_conv, state_h, *weights)
    return (y_prompt, y_sample, k_prompt, v_prompt, conv_prompt, h_prompt,
            k_sample, v_sample, conv_sample, h_sample)
```

```python
import functools
import math

import jax
import jax.numpy as jnp
from jax import lax
from jax.experimental import pallas as pl
from jax.experimental.pallas import tpu as pltpu

f32 = jnp.float32
bf16 = jnp.bfloat16

D_MODEL = 1024
BATCH = 4
SEQ = 4096
DEPTH = 4
DEC_BATCH = 128
PAST_LEN = 2048
PAGE_SIZE = 128
N_PAGES = PAST_LEN // PAGE_SIZE
LRU_W = 512
LRU_BLOCKS = 8
LRU_BS = 64
CONV_W = 4
C_GATE = 8.0
ATT_W = 512
QK_HD = 64
V_HD = 128
N_HEADS = 4
ROT_DIM = 16
ROPE_THETA = 500000.0
D_FF = 2816
P_DIM = 256
EPS = 1e-6
NEG = -1e30
QK_SCALE = QK_HD ** -0.5

N_PROMPT = BATCH * SEQ
TM = 512
N_TILES_P = N_PROMPT // TM
N_TILES = N_TILES_P + 1
N_PAD = N_TILES * TM
SEQ_TILES = SEQ // TM
FF_CHUNKS = 2
FF_C = D_FF // FF_CHUNKS
TQ = 512
TT = 512
SUB = 8
VMEM_LIMIT = 56 * 1024 * 1024


def _const_spec(shape):
    nd = len(shape)
    return pl.BlockSpec(shape, lambda *_: (0,) * nd, pipeline_mode=pl.Buffered(1))


def _layer_spec(shape, lead):
    nl = len(lead)
    nd = len(shape)
    return pl.BlockSpec((None,) * nl + tuple(shape), lambda *_: tuple(lead) + (0,) * nd,
                        pipeline_mode=pl.Buffered(1))


def _rms(x, g):
    ms = jnp.mean(x * x, axis=-1, keepdims=True)
    return x * lax.rsqrt(ms + EPS) * g


def _swiglu(hb, wg_ref, wu_ref, wd_ref):
    acc = None
    for c in range(FF_CHUNKS):
        sl = slice(c * FF_C, (c + 1) * FF_C)
        g = jnp.dot(hb, wg_ref[:, sl], preferred_element_type=f32)
        u = jnp.dot(hb, wu_ref[:, sl], preferred_element_type=f32)
        a = (g * jax.nn.sigmoid(g) * u).astype(bf16)
        y = jnp.dot(a, wd_ref[sl, :], preferred_element_type=f32)
        acc = y if acc is None else acc + y
    return acc


def _lam(wl_ref, lam_init):
    wl = wl_ref[...]
    s01 = jnp.sum(wl[0:1, :] * wl[1:2, :], axis=-1, keepdims=True)
    s23 = jnp.sum(wl[2:3, :] * wl[3:4, :], axis=-1, keepdims=True)
    return jnp.exp(s01) - jnp.exp(s23) + lam_init


def _rope_table_kernel(freq_ref, c_ref, sa_ref, sb_ref):
    rows = c_ref.shape[0]
    row = lax.broadcasted_iota(jnp.int32, (rows, V_HD), 0)
    lane = lax.broadcasted_iota(jnp.int32, (rows, V_HD), 1)
    pos = jnp.where(row < SEQ, row, PAST_LEN).astype(f32)
    ang = pos * freq_ref[...]
    cos = jnp.cos(ang)
    sin = jnp.sin(ang)
    in_comp = lane % QK_HD
    half = ROT_DIM // 2
    c_ref[...] = jnp.where(in_comp < ROT_DIM, cos, 1.0)
    sa_ref[...] = jnp.where(in_comp < half, -sin, 0.0)
    sb_ref[...] = jnp.where((in_comp >= half) & (in_comp < ROT_DIM), sin, 0.0)


def _rope_tables():
    half = ROT_DIM // 2
    freqs = jnp.power(jnp.float32(ROPE_THETA), -jnp.arange(0, ROT_DIM, 2, dtype=f32) / ROT_DIM)
    lane = jnp.arange(V_HD)
    freq_lane = freqs[(lane % QK_HD) % half].reshape(1, V_HD)
    rows = SEQ + TM
    shp = jax.ShapeDtypeStruct((rows, V_HD), f32)
    return pl.pallas_call(
        _rope_table_kernel,
        out_shape=(shp, shp, shp),
        name="rope_tables",
    )(freq_lane)


def _x_kernel(x_ref, gn0_ref, gn1_ref, wg_ref, wu_ref, wd_ref, win_ref, c_ref, sa_ref, sb_ref,
              xo_ref, u_ref, gate_ref, qb_ref, kb_ref, vb_ref, kf_ref, vf_ref):
    x = x_ref[...]
    h = _rms(x, gn0_ref[...]).astype(bf16)
    x1 = x + 0.5 * _swiglu(h, wg_ref, wu_ref, wd_ref)
    xo_ref[...] = x1
    h1 = _rms(x1, gn1_ref[...]).astype(bf16)

    def proj(i):
        return jnp.dot(h1, win_ref[:, i * LRU_W:(i + 1) * LRU_W], preferred_element_type=f32)

    u_ref[...] = proj(0)
    gate_ref[...] = proj(1)
    cos = c_ref[...]
    sa = sa_ref[...]
    sb = sb_ref[...]

    def rope(t):
        outs = []
        for hd in range(N_HEADS):
            th = t[:, hd * V_HD:(hd + 1) * V_HD]
            up = pltpu.roll(th, V_HD - ROT_DIM // 2, 1)
            dn = pltpu.roll(th, ROT_DIM // 2, 1)
            outs.append(th * cos + up * sa + dn * sb)
        return jnp.concatenate(outs, axis=-1)

    q = rope(proj(2))
    qb_ref[...] = q.astype(bf16)
    k = rope(proj(3))
    kf_ref[...] = k
    kb_ref[...] = k.astype(bf16)
    v = proj(4)
    vf_ref[...] = v
    vb_ref[...] = v.astype(bf16)


def _x_call(l, x, g_norm3, wg, wu, wd, win, tabs):
    tile = lambda w: pl.BlockSpec((TM, w), lambda i: (i, 0))
    tab_spec = pl.BlockSpec((TM, V_HD), lambda i: (jnp.where(i < N_TILES_P, i % SEQ_TILES, SEQ_TILES), 0))
    act = lambda dt: jax.ShapeDtypeStruct((N_PAD, LRU_W), dt)
    return pl.pallas_call(
        _x_kernel,
        grid=(N_TILES,),
        in_specs=[tile(D_MODEL),
                  _layer_spec((1, D_MODEL), (l, 0)), _layer_spec((1, D_MODEL), (l, 1)),
                  _layer_spec((D_MODEL, D_FF), (l, 0)), _layer_spec((D_MODEL, D_FF), (l, 0)),
                  _layer_spec((D_FF, D_MODEL), (l, 0)), _layer_spec((D_MODEL, 5 * LRU_W), (l,)),
                  tab_spec, tab_spec, tab_spec],
        out_specs=[tile(D_MODEL)] + [tile(LRU_W)] * 7,
        out_shape=[jax.ShapeDtypeStruct((N_PAD, D_MODEL), f32),
                   act(f32), act(f32), act(bf16), act(bf16), act(bf16), act(f32), act(f32)],
        compiler_params=pltpu.CompilerParams(dimension_semantics=("arbitrary",),
                                             vmem_limit_bytes=VMEM_LIMIT),
        name=f"ffn1_inproj_{l}",
    )(x, g_norm3, g_norm3, wg, wu, wd, win, *tabs)


def _y_kernel(x_ref, ol_ref, oa_ref, p_ref, wo_ref, gn2_ref, wg_ref, wu_ref, wd_ref,
              gn3_ref, wpg_ref, wpe_ref, *rest, final):
    x = x_ref[...]
    x2 = (x + jnp.dot(ol_ref[...], wo_ref[:LRU_W, :], preferred_element_type=f32)
          + jnp.dot(oa_ref[...], wo_ref[LRU_W:, :], preferred_element_type=f32))
    h = _rms(x2, gn2_ref[...]).astype(bf16)
    x3 = x2 + 0.5 * _swiglu(h, wg_ref, wu_ref, wd_ref)
    h3 = _rms(x3, gn3_ref[...]).astype(bf16)
    gate = jax.nn.sigmoid(jnp.dot(h3, wpg_ref[...], preferred_element_type=f32))
    pe = jnp.dot(p_ref[...].astype(bf16), wpe_ref[...], preferred_element_type=f32)
    x4 = x3 + gate * pe
    if final:
        gf_ref, y_ref = rest
        y_ref[...] = _rms(x4, gf_ref[...])
    else:
        (xo_ref,) = rest
        xo_ref[...] = x4


def _y_call(l, x, o_lru, o_attn, p_all, wo, g_norm3, wg, wu, wd, wpg, wpe, g_final):
    final = l == DEPTH - 1
    tile = lambda w: pl.BlockSpec((TM, w), lambda i: (i, 0))
    in_specs = [tile(D_MODEL), tile(LRU_W), tile(ATT_W),
                pl.BlockSpec((None, TM, P_DIM), lambda i: (l, i, 0)),
                _layer_spec((D_MODEL, D_MODEL), (l,)),
                _layer_spec((1, D_MODEL), (l, 2)),
                _layer_spec((D_MODEL, D_FF), (l, 1)), _layer_spec((D_MODEL, D_FF), (l, 1)),
                _layer_spec((D_FF, D_MODEL), (l, 1)),
                _layer_spec((1, D_MODEL), (l, 3)),
                _layer_spec((D_MODEL, D_MODEL), (l,)), _layer_spec((P_DIM, D_MODEL), (l,))]
    args = [x, o_lru, o_attn, p_all, wo, g_norm3, wg, wu, wd, g_norm3, wpg, wpe]
    if final:
        in_specs.append(_const_spec((1, D_MODEL)))
        args.append(g_final)
    return pl.pallas_call(
        functools.partial(_y_kernel, final=final),
        grid=(N_TILES,),
        in_specs=in_specs,
        out_specs=tile(D_MODEL),
        out_shape=jax.ShapeDtypeStruct((N_PAD, D_MODEL), f32),
        compiler_params=pltpu.CompilerParams(dimension_semantics=("arbitrary",),
                                             vmem_limit_bytes=VMEM_LIMIT),
        name=f"outproj_ffn2_embed_{l}",
    )(*args)


def _softplus(x):
    return jnp.maximum(x, 0.0) + jnp.log1p(jnp.exp(-jnp.abs(x)))


def _lru_gates(xc, wa_ref, wi_ref, ba_ref, bi_ref, ap_ref):
    xb = xc.astype(bf16)
    half = LRU_W // 2

    def bd(w_ref):
        lo = jnp.dot(xb[:, :half], w_ref[0], preferred_element_type=f32)
        hi = jnp.dot(xb[:, half:], w_ref[1], preferred_element_type=f32)
        return jnp.concatenate([lo, hi], axis=-1)

    r = jax.nn.sigmoid(bd(wa_ref) + ba_ref[...])
    i = jax.nn.sigmoid(bd(wi_ref) + bi_ref[...])
    log_a = -C_GATE * r * _softplus(-ap_ref[...])
    a = jnp.exp(log_a)
    b = jnp.sqrt(-jnp.tanh(log_a) * (a * a + 1.0)) * (i * xc)
    return a, b


def _lru_prompt_kernel(u_ref, gate_ref, cw_ref, cb_ref, wa_ref, wi_ref, ba_ref, bi_ref, ap_ref,
                       oinit_ref, o_ref, conv_ref, hlast_ref, ubuf, abuf, bbuf, hcar):
    del oinit_ref
    t = pl.program_id(1)

    @pl.when(t == 0)
    def _():
        ubuf[0:SUB, :] = jnp.zeros((SUB, LRU_W), f32)
        hcar[...] = jnp.zeros((1, LRU_W), f32)

    ubuf[SUB:SUB + TT, :] = u_ref[...]
    xc = cb_ref[...]
    for j in range(CONV_W):
        off = SUB - (CONV_W - 1) + j
        xc = xc + ubuf[off:off + TT, :] * cw_ref[j:j + 1, :]
    conv_ref[...] = ubuf[TT + SUB - (CONV_W - 1):TT + SUB, :]
    ubuf[0:SUB, :] = ubuf[TT:TT + SUB, :]

    a, b = _lru_gates(xc, wa_ref, wi_ref, ba_ref, bi_ref, ap_ref)
    abuf[...] = a
    bbuf[...] = b

    row = lax.broadcasted_iota(jnp.int32, (SUB, LRU_W), 0)

    def group(g, h_prev):
        off = pl.multiple_of(g * SUB, SUB)
        ag = abuf[pl.ds(off, SUB), :]
        bg = bbuf[pl.ds(off, SUB), :]
        for d in (1, 2, 4):
            keep = row >= d
            a_sh = pltpu.roll(ag, d, 0)
            b_sh = pltpu.roll(bg, d, 0)
            bg = jnp.where(keep, ag * b_sh + bg, bg)
            ag = jnp.where(keep, ag * a_sh, ag)
        hg = ag * h_prev + bg
        bbuf[pl.ds(off, SUB), :] = hg
        return hg[SUB - 1:SUB, :]

    h_last = lax.fori_loop(0, TT // SUB, group, hcar[...])
    hcar[...] = h_last
    hlast_ref[...] = h_last
    o_ref[...] = (bbuf[...] * jax.nn.gelu(gate_ref[...])).astype(bf16)


def _lru_prompt_call(l, u, gate, o_init, cw, cb, wa, wi, ba, bi, ap):
    tile = pl.BlockSpec((TT, LRU_W), lambda b, t: (b * (SEQ // TT) + t, 0))
    vec = lambda: pl.BlockSpec((None, 1, LRU_W), lambda b, t: (l, 0, 0))
    return pl.pallas_call(
        _lru_prompt_kernel,
        grid=(BATCH, SEQ // TT),
        in_specs=[tile, tile,
                  pl.BlockSpec((None, CONV_W, LRU_W), lambda b, t: (l, 0, 0)), vec(),
                  pl.BlockSpec((None, 2, LRU_W // 2, LRU_W // 2), lambda b, t: (l, 0, 0, 0)),
                  pl.BlockSpec((None, 2, LRU_W // 2, LRU_W // 2), lambda b, t: (l, 0, 0, 0)),
                  vec(), vec(), vec(),
                  pl.BlockSpec(memory_space=pl.ANY)],
        out_specs=[tile,
                   pl.BlockSpec((None, CONV_W - 1, LRU_W), lambda b, t: (b, 0, 0)),
                   pl.BlockSpec((None, 1, LRU_W), lambda b, t: (b, 0, 0))],
        out_shape=[jax.ShapeDtypeStruct((N_PAD, LRU_W), bf16),
                   jax.ShapeDtypeStruct((BATCH, CONV_W - 1, LRU_W), f32),
                   jax.ShapeDtypeStruct((BATCH, 1, LRU_W), f32)],
        scratch_shapes=[pltpu.VMEM((TT + SUB, LRU_W), f32), pltpu.VMEM((TT, LRU_W), f32),
                        pltpu.VMEM((TT, LRU_W), f32), pltpu.VMEM((1, LRU_W), f32)],
        input_output_aliases={9: 0},
        compiler_params=pltpu.CompilerParams(dimension_semantics=("arbitrary", "arbitrary"),
                                             vmem_limit_bytes=VMEM_LIMIT),
        name=f"rglru_prompt_{l}",
    )(u, gate, cw, cb, wa, wi, ba, bi, ap, o_init)


def _lru_decode_kernel(u_ref, gate_ref, sc_ref, h0_ref, cw_ref, cb_ref, wa_ref, wi_ref,
                       ba_ref, bi_ref, ap_ref, o_ref, conv_ref, h_ref):
    u = u_ref[...]
    xc = cb_ref[...]
    for j in range(CONV_W - 1):
        xc = xc + sc_ref[j] * cw_ref[j:j + 1, :]
    xc = xc + u * cw_ref[CONV_W - 1:CONV_W, :]
    a, b = _lru_gates(xc, wa_ref, wi_ref, ba_ref, bi_ref, ap_ref)
    h = a * h0_ref[...] + b
    h_ref[...] = h
    for j in range(CONV_W - 2):
        conv_ref[j] = sc_ref[j + 1]
    conv_ref[CONV_W - 2] = u
    o_ref[...] = (h * jax.nn.gelu(gate_ref[...])).astype(bf16)


def _lru_decode_call(l, u, gate, sc_t, h0, cw, cb, wa, wi, ba, bi, ap):
    rows = pl.BlockSpec((DEC_BATCH, LRU_W), lambda i: (N_PROMPT // DEC_BATCH, 0))
    vec = lambda: pl.BlockSpec((None, 1, LRU_W), lambda i: (l, 0, 0))
    return pl.pallas_call(
        _lru_decode_kernel,
        grid=(1,),
        in_specs=[rows, rows,
                  pl.BlockSpec((None, CONV_W - 1, DEC_BATCH, LRU_W), lambda i: (l, 0, 0, 0)),
                  pl.BlockSpec((None, DEC_BATCH, LRU_W), lambda i: (l, 0, 0)),
                  pl.BlockSpec((None, CONV_W, LRU_W), lambda i: (l, 0, 0)), vec(),
                  pl.BlockSpec((None, 2, LRU_W // 2, LRU_W // 2), lambda i: (l, 0, 0, 0)),
                  pl.BlockSpec((None, 2, LRU_W // 2, LRU_W // 2), lambda i: (l, 0, 0, 0)),
                  vec(), vec(), vec()],
        out_specs=[pl.BlockSpec((DEC_BATCH, LRU_W), lambda i: (0, 0)),
                   pl.BlockSpec((CONV_W - 1, DEC_BATCH, LRU_W), lambda i: (0, 0, 0)),
                   pl.BlockSpec((DEC_BATCH, LRU_W), lambda i: (0, 0))],
        out_shape=[jax.ShapeDtypeStruct((DEC_BATCH, LRU_W), bf16),
                   jax.ShapeDtypeStruct((CONV_W - 1, DEC_BATCH, LRU_W), f32),
                   jax.ShapeDtypeStruct((DEC_BATCH, LRU_W), f32)],
        name=f"rglru_decode_{l}",
    )(u, gate, sc_t, h0, cw, cb, wa, wi, ba, bi, ap)


def _subln(o, gs, lam_init):
    ms = jnp.mean(o * o, axis=-1, keepdims=True)
    return (o * lax.rsqrt(ms + EPS) * gs) * (1.0 - lam_init)


def _attn_prompt_kernel(q_ref, k_ref, v_ref, wl_ref, gs_ref, oinit_ref, o_ref,
                        m1, l1, a1, m2, l2, a2, *, lam_init):
    del oinit_ref
    qi = pl.program_id(2)
    q = q_ref[...]
    lane = lax.broadcasted_iota(jnp.int32, (TQ, V_HD), 1)
    zero = jnp.zeros_like(q)
    scale = jnp.asarray(QK_SCALE, bf16)
    q1 = jnp.where(lane < QK_HD, q, zero) * scale
    q2 = jnp.where(lane >= QK_HD, q, zero) * scale
    for m_ref, l_ref, a_ref in ((m1, l1, a1), (m2, l2, a2)):
        m_ref[...] = jnp.full((TQ, 1), NEG, f32)
        l_ref[...] = jnp.zeros((TQ, 1), f32)
        a_ref[...] = jnp.zeros((TQ, V_HD), f32)

    def step(j, masked):
        off = pl.multiple_of(j * TQ, TQ)
        kb = k_ref[pl.ds(off, TQ), :]
        vb = v_ref[pl.ds(off, TQ), :]
        for qq, m_ref, l_ref, a_ref in ((q1, m1, l1, a1), (q2, m2, l2, a2)):
            s = lax.dot_general(qq, kb, (((1,), (1,)), ((), ())), preferred_element_type=f32)
            if masked:
                r = lax.broadcasted_iota(jnp.int32, (TQ, TQ), 0)
                c = lax.broadcasted_iota(jnp.int32, (TQ, TQ), 1)
                s = jnp.where(c <= r, s, NEG)
            m_old = m_ref[...]
            m_new = jnp.maximum(m_old, jnp.max(s, axis=1, keepdims=True))
            alpha = jnp.exp(m_old - m_new)
            p = jnp.exp(s - m_new)
            l_ref[...] = alpha * l_ref[...] + jnp.sum(p, axis=1, keepdims=True)
            a_ref[...] = alpha * a_ref[...] + jnp.dot(p.astype(bf16), vb, preferred_element_type=f32)
            m_ref[...] = m_new

    def body(j, carry):
        step(j, False)
        return carry

    lax.fori_loop(0, qi, body, 0)
    step(qi, True)
    lam = _lam(wl_ref, lam_init)
    o = a1[...] / l1[...] - lam * (a2[...] / l2[...])
    o_ref[...] = _subln(o, gs_ref[...], lam_init).astype(bf16)


def _attn_prompt_call(l, qb, kb, vb, o_init, w_lambda, g_subln3):
    lam_init = 0.8 - 0.6 * math.exp(-0.3 * l)
    nq = SEQ // TQ
    kv_spec = pl.BlockSpec((SEQ, V_HD), lambda b, h, i: (b, h))
    qo_spec = pl.BlockSpec((TQ, V_HD), lambda b, h, i: (b * nq + i, h))
    return pl.pallas_call(
        functools.partial(_attn_prompt_kernel, lam_init=lam_init),
        grid=(BATCH, N_HEADS, nq),
        in_specs=[qo_spec, kv_spec, kv_spec,
                  pl.BlockSpec((None, 4, QK_HD), lambda b, h, i: (l, 0, 0)),
                  pl.BlockSpec((None, 1, V_HD), lambda b, h, i: (l, 0, 0)),
                  pl.BlockSpec(memory_space=pl.ANY)],
        out_specs=qo_spec,
        out_shape=jax.ShapeDtypeStruct((N_PAD, ATT_W), bf16),
        scratch_shapes=[pltpu.VMEM((TQ, 1), f32), pltpu.VMEM((TQ, 1), f32), pltpu.VMEM((TQ, V_HD), f32),
                        pltpu.VMEM((TQ, 1), f32), pltpu.VMEM((TQ, 1), f32), pltpu.VMEM((TQ, V_HD), f32)],
        input_output_aliases={5: 0},
        compiler_params=pltpu.CompilerParams(
            dimension_semantics=("arbitrary", "arbitrary", "arbitrary"),
            vmem_limit_bytes=VMEM_LIMIT),
        name=f"attn_prompt_{l}",
    )(qb, kb, vb, w_lambda, g_subln3, o_init)


def _attn_decode_kernel(pt_ref, q_ref, kn_ref, vn_ref, wl_ref, gs_ref, *rest, lam_init):
    del pt_ref
    k_refs = rest[:N_PAGES]
    v_refs = rest[N_PAGES:2 * N_PAGES]
    o_ref = rest[2 * N_PAGES]
    nc = 2 * N_HEADS
    row = lax.broadcasted_iota(jnp.int32, (nc, ATT_W), 0)
    lane = lax.broadcasted_iota(jnp.int32, (nc, ATT_W), 1)
    q = jnp.broadcast_to(q_ref[...].astype(f32), (nc, ATT_W))
    q8 = (jnp.where(jnp.right_shift(lane, 6) == row, q, 0.0) * QK_SCALE).astype(bf16)

    s_pages = [lax.dot_general(q8, k_refs[p][...].astype(bf16), (((1,), (1,)), ((), ())),
                               preferred_element_type=f32) for p in range(N_PAGES)]
    s = jnp.concatenate(s_pages, axis=-1)
    kn = kn_ref[...].astype(bf16).astype(f32)
    vn = vn_ref[...].astype(bf16).astype(f32)
    s_new = jnp.sum(q8.astype(f32) * kn, axis=-1, keepdims=True)
    m = jnp.maximum(jnp.max(s, axis=-1, keepdims=True), s_new)
    e = jnp.exp(s - m)
    e_new = jnp.exp(s_new - m)
    denom = jnp.sum(e, axis=-1, keepdims=True) + e_new
    eb = e.astype(bf16)
    acc = e_new.astype(bf16).astype(f32) * vn
    for p in range(N_PAGES):
        acc = acc + jnp.dot(eb[:, p * PAGE_SIZE:(p + 1) * PAGE_SIZE], v_refs[p][...].astype(bf16),
                            preferred_element_type=f32)
    lam = _lam(wl_ref, lam_init)
    r1 = lax.broadcasted_iota(jnp.int32, (nc, 1), 0)
    coef = jnp.where(jnp.bitwise_and(r1, 1) == 0, 1.0, -lam) / denom
    own = jnp.right_shift(lane, 7) == jnp.right_shift(row, 1)
    o = jnp.sum(jnp.where(own, acc * coef, 0.0), axis=0, keepdims=True)
    gs = gs_ref[...]
    outs = [_subln(o[:, h * V_HD:(h + 1) * V_HD], gs, lam_init) for h in range(N_HEADS)]
    o_ref[...] = jnp.concatenate(outs, axis=-1).astype(bf16)


def _attn_decode_call(l, page_table, q_s, k_s, v_s, cache_k, cache_v, w_lambda, g_subln3):
    lam_init = 0.8 - 0.6 * math.exp(-0.3 * l)
    row_spec = pl.BlockSpec((None, 1, ATT_W), lambda b, pt: (b, 0, 0))

    def page_spec(p):
        return pl.BlockSpec((None, None, PAGE_SIZE, ATT_W), lambda b, pt: (l, pt[b, p], 0, 0))

    grid_spec = pltpu.PrefetchScalarGridSpec(
        num_scalar_prefetch=1,
        grid=(DEC_BATCH,),
        in_specs=[row_spec, row_spec, row_spec,
                  pl.BlockSpec((None, 4, QK_HD), lambda b, pt: (l, 0, 0)),
                  pl.BlockSpec((None, 1, V_HD), lambda b, pt: (l, 0, 0))]
                 + [page_spec(p) for p in range(N_PAGES)] * 2,
        out_specs=row_spec,
    )
    return pl.pallas_call(
        functools.partial(_attn_decode_kernel, lam_init=lam_init),
        grid_spec=grid_spec,
        out_shape=jax.ShapeDtypeStruct((DEC_BATCH, 1, ATT_W), bf16),
        compiler_params=pltpu.CompilerParams(dimension_semantics=("arbitrary",),
                                             vmem_limit_bytes=VMEM_LIMIT),
        name=f"attn_decode_{l}",
    )(page_table, q_s, k_s, v_s, w_lambda, g_subln3,
      *([cache_k] * N_PAGES), *([cache_v] * N_PAGES))


def _block_diag(w):
    per = LRU_BLOCKS // 2
    w = w.reshape(DEPTH, 2, per, LRU_BS, LRU_BS)
    eye = jnp.eye(per, dtype=w.dtype)
    return jnp.einsum('dgnij,nm->dgnimj', w, eye).reshape(DEPTH, 2, per * LRU_BS, per * LRU_BS)


@jax.jit
def kernel(x_prompt, x_sample, cache_k, cache_v, page_table, state_conv, state_h, p_prompt, p_sample,
           w_in, w_out, conv_w, conv_b, w_a, b_a, w_i, b_i, a_param, w_lambda, g_subln,
           w_ffn_gate, w_ffn_up, w_ffn_down, w_pe, w_pg, g_norm, g_final):
    n_dec = DEC_BATCH
    pad_rows = N_PAD - N_PROMPT - n_dec
    x = jnp.concatenate([x_prompt.reshape(N_PROMPT, D_MODEL), x_sample.reshape(n_dec, D_MODEL),
                         jnp.zeros((pad_rows, D_MODEL), f32)], axis=0)
    p_all = jnp.concatenate([p_prompt.reshape(DEPTH, N_PROMPT, P_DIM), p_sample.reshape(DEPTH, n_dec, P_DIM),
                             jnp.zeros((DEPTH, pad_rows, P_DIM), f32)], axis=1)

    wg_b = w_ffn_gate.astype(bf16)
    wu_b = w_ffn_up.astype(bf16)
    wd_b = w_ffn_down.astype(bf16)
    win_b = w_in.astype(bf16)
    wo_b = w_out.astype(bf16)
    wpg_b = w_pg.astype(bf16)
    wpe_b = w_pe.astype(bf16)
    wa_b = _block_diag(w_a).astype(bf16)
    wi_b = _block_diag(w_i).astype(bf16)
    g_norm3 = g_norm.reshape(DEPTH, 4, 1, D_MODEL)
    g_final2 = g_final.reshape(1, D_MODEL)
    g_subln3 = g_subln.reshape(DEPTH, 1, V_HD)
    cb3 = conv_b.reshape(DEPTH, 1, LRU_W)
    ba3 = b_a.reshape(DEPTH, 1, LRU_W)
    bi3 = b_i.reshape(DEPTH, 1, LRU_W)
    ap3 = a_param.reshape(DEPTH, 1, LRU_W)
    sc_t = state_conv.transpose(0, 2, 1, 3)
    ck = cache_k.reshape(DEPTH, -1, PAGE_SIZE, ATT_W)
    cv = cache_v.reshape(DEPTH, -1, PAGE_SIZE, ATT_W)
    tabs = _rope_tables()

    ks, vs, convs_p, hs_p, convs_s, hs_s = [], [], [], [], [], []
    y = None
    dec = slice(N_PROMPT, N_PROMPT + n_dec)
    for l in range(DEPTH):
        x, u, gate, qb, kb, vb, kf, vf = _x_call(l, x, g_norm3, wg_b, wu_b, wd_b, win_b, tabs)
        ks.append(kf)
        vs.append(vf)

        o_lru, conv_p, h_p = _lru_prompt_call(l, u, gate, jnp.zeros((N_PAD, LRU_W), bf16),
                                              conv_w, cb3, wa_b, wi_b, ba3, bi3, ap3)
        o_lru_s, conv_s, h_s = _lru_decode_call(l, u, gate, sc_t, state_h,
                                                conv_w, cb3, wa_b, wi_b, ba3, bi3, ap3)
        o_lru = lax.dynamic_update_slice(o_lru, o_lru_s, (N_PROMPT, 0))
        convs_p.append(conv_p)
        hs_p.append(h_p.reshape(BATCH, LRU_W))
        convs_s.append(conv_s.transpose(1, 0, 2))
        hs_s.append(h_s)

        o_attn = _attn_prompt_call(l, qb, kb, vb, jnp.zeros((N_PAD, ATT_W), bf16), w_lambda, g_subln3)
        o_attn_s = _attn_decode_call(l, page_table, qb[dec].reshape(n_dec, 1, ATT_W),
                                     kf[dec].reshape(n_dec, 1, ATT_W), vf[dec].reshape(n_dec, 1, ATT_W),
                                     ck, cv, w_lambda, g_subln3)
        o_attn = lax.dynamic_update_slice(o_attn, o_attn_s.reshape(n_dec, ATT_W), (N_PROMPT, 0))

        x = _y_call(l, x, o_lru, o_attn, p_all, wo_b, g_norm3, wg_b, wu_b, wd_b, wpg_b, wpe_b, g_final2)
    y = x

    prm = slice(0, N_PROMPT)
    y_prompt = y[prm].reshape(BATCH, SEQ, D_MODEL)
    y_sample = y[dec].reshape(n_dec, 1, D_MODEL)
    k_prompt = jnp.stack([k[prm].reshape(BATCH, SEQ, N_HEADS, V_HD) for k in ks])
    v_prompt = jnp.stack([v[prm].reshape(BATCH, SEQ, N_HEADS, V_HD) for v in vs])
    k_sample = jnp.stack([k[dec].reshape(n_dec, 1, N_HEADS, V_HD) for k in ks])
    v_sample = jnp.stack([v[dec].reshape(n_dec, 1, N_HEADS, V_HD) for v in vs])
    return (y_prompt, y_sample, k_prompt, v_prompt, jnp.stack(convs_p), jnp.stack(hs_p),
            k_sample, v_sample, jnp.stack(convs_s), jnp.stack(hs_s))
```

```python
import functools
import math

import jax
import jax.numpy as jnp
from jax import lax
from jax.experimental import pallas as pl
from jax.experimental.pallas import tpu as pltpu

f32 = jnp.float32
bf16 = jnp.bfloat16

D_MODEL = 1024
BATCH = 4
SEQ = 4096
DEPTH = 4
DEC_BATCH = 128
PAST_LEN = 2048
PAGE_SIZE = 128
N_PAGES = PAST_LEN // PAGE_SIZE
N_HEADS = 4
PAGE_ROWS = PAGE_SIZE * N_HEADS
LRU_W = 512
LRU_BLOCKS = 8
LRU_BS = 64
CONV_W = 4
C_GATE = 8.0
ATT_W = 512
QK_HD = 64
V_HD = 128
ROT_DIM = 16
ROPE_THETA = 500000.0
D_FF = 2816
P_DIM = 256
EPS = 1e-6
NEG = -1e30
QK_SCALE = QK_HD ** -0.5

N_PROMPT = BATCH * SEQ
TM = 512
N_TILES_P = N_PROMPT // TM
N_TILES = N_TILES_P + 1
N_PAD = N_TILES * TM
SEQ_TILES = SEQ // TM
FF_CHUNKS = 2
FF_C = D_FF // FF_CHUNKS
TQ = 512
TT = 512
SUB = 8
VMEM_LIMIT = 56 * 1024 * 1024


def _const_spec(shape):
    nd = len(shape)
    return pl.BlockSpec(shape, lambda *_: (0,) * nd, pipeline_mode=pl.Buffered(1))


def _layer_spec(shape, lead):
    nl = len(lead)
    nd = len(shape)
    return pl.BlockSpec((None,) * nl + tuple(shape), lambda *_: tuple(lead) + (0,) * nd,
                        pipeline_mode=pl.Buffered(1))


def _rms(x, g):
    ms = jnp.mean(x * x, axis=-1, keepdims=True)
    return x * lax.rsqrt(ms + EPS) * g


def _swiglu(hb, wg_ref, wu_ref, wd_ref):
    acc = None
    for c in range(FF_CHUNKS):
        sl = slice(c * FF_C, (c + 1) * FF_C)
        g = jnp.dot(hb, wg_ref[:, sl], preferred_element_type=f32)
        u = jnp.dot(hb, wu_ref[:, sl], preferred_element_type=f32)
        a = (g * jax.nn.sigmoid(g) * u).astype(bf16)
        y = jnp.dot(a, wd_ref[sl, :], preferred_element_type=f32)
        acc = y if acc is None else acc + y
    return acc


def _lam(wl_ref, lam_init):
    wl = wl_ref[...]
    s01 = jnp.sum(wl[0:1, :] * wl[1:2, :], axis=-1, keepdims=True)
    s23 = jnp.sum(wl[2:3, :] * wl[3:4, :], axis=-1, keepdims=True)
    return jnp.exp(s01) - jnp.exp(s23) + lam_init


def _rope_table_kernel(freq_ref, c_ref, sa_ref, sb_ref):
    rows = c_ref.shape[0]
    row = lax.broadcasted_iota(jnp.int32, (rows, V_HD), 0)
    lane = lax.broadcasted_iota(jnp.int32, (rows, V_HD), 1)
    pos = jnp.where(row < SEQ, row, PAST_LEN).astype(f32)
    ang = pos * freq_ref[...]
    cos = jnp.cos(ang)
    sin = jnp.sin(ang)
    in_comp = lane % QK_HD
    half = ROT_DIM // 2
    c_ref[...] = jnp.where(in_comp < ROT_DIM, cos, 1.0)
    sa_ref[...] = jnp.where(in_comp < half, -sin, 0.0)
    sb_ref[...] = jnp.where((in_comp >= half) & (in_comp < ROT_DIM), sin, 0.0)


def _rope_tables():
    half = ROT_DIM // 2
    freqs = jnp.power(jnp.float32(ROPE_THETA), -jnp.arange(0, ROT_DIM, 2, dtype=f32) / ROT_DIM)
    lane = jnp.arange(V_HD)
    freq_lane = freqs[(lane % QK_HD) % half].reshape(1, V_HD)
    rows = SEQ + TM
    shp = jax.ShapeDtypeStruct((rows, V_HD), f32)
    return pl.pallas_call(
        _rope_table_kernel,
        out_shape=(shp, shp, shp),
        name="rope_tables",
    )(freq_lane)


def _store_heads_interleaved(dst_ref, t):
    for hd in range(N_HEADS):
        dst_ref[pl.ds(hd, TM, stride=N_HEADS), :] = t[:, hd * V_HD:(hd + 1) * V_HD]


def _x_kernel(x_ref, gn0_ref, gn1_ref, wg_ref, wu_ref, wd_ref, win_ref, c_ref, sa_ref, sb_ref, *rest):
    (xo_ref, u_ref, gate_ref, qs_ref, qt_ref, kb_ref, vt_ref,
     kfin_ref, vfin_ref, ksn_ref, vsn_ref) = rest[-11:]
    i = pl.program_id(0)
    x = x_ref[...]
    h = _rms(x, gn0_ref[...]).astype(bf16)
    x1 = x + 0.5 * _swiglu(h, wg_ref, wu_ref, wd_ref)
    xo_ref[...] = x1
    h1 = _rms(x1, gn1_ref[...]).astype(bf16)

    def proj(i):
        return jnp.dot(h1, win_ref[:, i * LRU_W:(i + 1) * LRU_W], preferred_element_type=f32)

    u_ref[...] = proj(0)
    gate_ref[...] = proj(1)
    cos = c_ref[...]
    sa = sa_ref[...]
    sb = sb_ref[...]

    def rope(t):
        outs = []
        for hd in range(N_HEADS):
            th = t[:, hd * V_HD:(hd + 1) * V_HD]
            up = pltpu.roll(th, V_HD - ROT_DIM // 2, 1)
            dn = pltpu.roll(th, ROT_DIM // 2, 1)
            outs.append(th * cos + up * sa + dn * sb)
        return jnp.concatenate(outs, axis=-1)

    q = rope(proj(2))
    qt_ref[...] = q.T.astype(bf16)
    k = rope(proj(3))
    kb_ref[...] = k.astype(bf16)
    v = proj(4)
    vt_ref[...] = v.T.astype(bf16)

    @pl.when(i < N_TILES_P)
    def _():
        _store_heads_interleaved(kfin_ref, k)
        _store_heads_interleaved(vfin_ref, v)

    @pl.when(i == N_TILES_P)
    def _():
        qs_ref[...] = q.astype(bf16)
        _store_heads_interleaved(ksn_ref, k)
        _store_heads_interleaved(vsn_ref, v)


def _x_call(l, x, g_norm3, wg, wu, wd, win, tabs, kfin, vfin):
    tile = lambda w: pl.BlockSpec((TM, w), lambda i: (i, 0))
    tile_t = pl.BlockSpec((LRU_W, TM), lambda i: (0, i))
    tab_spec = pl.BlockSpec((TM, V_HD), lambda i: (jnp.where(i < N_TILES_P, i % SEQ_TILES, SEQ_TILES), 0))
    fin_spec = pl.BlockSpec((None, TM * N_HEADS, V_HD), lambda i: (l, jnp.minimum(i, N_TILES_P - 1), 0))
    dec_spec = lambda w: pl.BlockSpec((TM * N_HEADS, w), lambda i: (0, 0))
    act = lambda dt: jax.ShapeDtypeStruct((N_PAD, LRU_W), dt)
    act_t = jax.ShapeDtypeStruct((LRU_W, N_PAD), bf16)
    fin = jax.ShapeDtypeStruct((DEPTH, N_PROMPT * N_HEADS, V_HD), f32)
    dec = jax.ShapeDtypeStruct((TM * N_HEADS, V_HD), f32)
    in_specs = [tile(D_MODEL),
                _layer_spec((1, D_MODEL), (l, 0)), _layer_spec((1, D_MODEL), (l, 1)),
                _layer_spec((D_MODEL, D_FF), (l, 0)), _layer_spec((D_MODEL, D_FF), (l, 0)),
                _layer_spec((D_FF, D_MODEL), (l, 0)), _layer_spec((D_MODEL, 5 * LRU_W), (l,)),
                tab_spec, tab_spec, tab_spec]
    args = [x, g_norm3, g_norm3, wg, wu, wd, win, *tabs]
    aliases = {}
    if kfin is not None:
        aliases = {len(args): 7, len(args) + 1: 8}
        in_specs += [pl.BlockSpec(memory_space=pl.ANY)] * 2
        args += [kfin, vfin]
    return pl.pallas_call(
        _x_kernel,
        grid=(N_TILES,),
        in_specs=in_specs,
        out_specs=[tile(D_MODEL), tile(LRU_W), tile(LRU_W),
                   pl.BlockSpec((TM, ATT_W), lambda i: (0, 0)), tile_t, tile(ATT_W), tile_t,
                   fin_spec, fin_spec, dec_spec(V_HD), dec_spec(V_HD)],
        out_shape=[jax.ShapeDtypeStruct((N_PAD, D_MODEL), f32), act(f32), act(f32),
                   jax.ShapeDtypeStruct((TM, ATT_W), bf16), act_t, act(bf16), act_t,
                   fin, fin, dec, dec],
        input_output_aliases=aliases,
        compiler_params=pltpu.CompilerParams(dimension_semantics=("arbitrary",),
                                             vmem_limit_bytes=VMEM_LIMIT),
        name=f"ffn1_inproj_{l}",
    )(*args)


def _y_kernel(x_ref, ol_ref, oa_ref, p_ref, wo_ref, gn2_ref, wg_ref, wu_ref, wd_ref,
              gn3_ref, wpg_ref, wpe_ref, *rest, final):
    x = x_ref[...]
    x2 = (x + jnp.dot(ol_ref[...], wo_ref[:LRU_W, :], preferred_element_type=f32)
          + lax.dot_general(oa_ref[...], wo_ref[LRU_W:, :], (((0,), (0,)), ((), ())),
                            preferred_element_type=f32))
    h = _rms(x2, gn2_ref[...]).astype(bf16)
    x3 = x2 + 0.5 * _swiglu(h, wg_ref, wu_ref, wd_ref)
    h3 = _rms(x3, gn3_ref[...]).astype(bf16)
    gate = jax.nn.sigmoid(jnp.dot(h3, wpg_ref[...], preferred_element_type=f32))
    pe = jnp.dot(p_ref[...].astype(bf16), wpe_ref[...], preferred_element_type=f32)
    x4 = x3 + gate * pe
    if final:
        gf_ref, y_ref = rest
        y_ref[...] = _rms(x4, gf_ref[...])
    else:
        (xo_ref,) = rest
        xo_ref[...] = x4


def _y_call(l, x, o_lru, o_attn, p_all, wo, g_norm3, wg, wu, wd, wpg, wpe, g_final):
    final = l == DEPTH - 1
    tile = lambda w: pl.BlockSpec((TM, w), lambda i: (i, 0))
    in_specs = [tile(D_MODEL), tile(LRU_W), pl.BlockSpec((ATT_W, TM), lambda i: (0, i)),
                pl.BlockSpec((None, TM, P_DIM), lambda i: (l, i, 0)),
                _layer_spec((D_MODEL, D_MODEL), (l,)),
                _layer_spec((1, D_MODEL), (l, 2)),
                _layer_spec((D_MODEL, D_FF), (l, 1)), _layer_spec((D_MODEL, D_FF), (l, 1)),
                _layer_spec((D_FF, D_MODEL), (l, 1)),
                _layer_spec((1, D_MODEL), (l, 3)),
                _layer_spec((D_MODEL, D_MODEL), (l,)), _layer_spec((P_DIM, D_MODEL), (l,))]
    args = [x, o_lru, o_attn, p_all, wo, g_norm3, wg, wu, wd, g_norm3, wpg, wpe]
    if final:
        in_specs.append(_const_spec((1, D_MODEL)))
        args.append(g_final)
    return pl.pallas_call(
        functools.partial(_y_kernel, final=final),
        grid=(N_TILES,),
        in_specs=in_specs,
        out_specs=tile(D_MODEL),
        out_shape=jax.ShapeDtypeStruct((N_PAD, D_MODEL), f32),
        compiler_params=pltpu.CompilerParams(dimension_semantics=("arbitrary",),
                                             vmem_limit_bytes=VMEM_LIMIT),
        name=f"outproj_ffn2_embed_{l}",
    )(*args)


def _softplus(x):
    return jnp.maximum(x, 0.0) + jnp.log1p(jnp.exp(-jnp.abs(x)))


def _lru_gates(xc, wa_ref, wi_ref, ba_ref, bi_ref, ap_ref):
    xb = xc.astype(bf16)
    half = LRU_W // 2

    def bd(w_ref):
        lo = jnp.dot(xb[:, :half], w_ref[0], preferred_element_type=f32)
        hi = jnp.dot(xb[:, half:], w_ref[1], preferred_element_type=f32)
        return jnp.concatenate([lo, hi], axis=-1)

    r = jax.nn.sigmoid(bd(wa_ref) + ba_ref[...])
    i = jax.nn.sigmoid(bd(wi_ref) + bi_ref[...])
    log_a = -C_GATE * r * _softplus(-ap_ref[...])
    a = jnp.exp(log_a)
    b = jnp.sqrt(-jnp.tanh(log_a) * (a * a + 1.0)) * (i * xc)
    return a, b


def _lru_prompt_kernel(u_ref, gate_ref, cw_ref, cb_ref, wa_ref, wi_ref, ba_ref, bi_ref, ap_ref,
                       oinit_ref, o_ref, conv_ref, hlast_ref, ubuf, abuf, bbuf, hcar):
    del oinit_ref
    t = pl.program_id(1)

    @pl.when(t == 0)
    def _():
        ubuf[0:SUB, :] = jnp.zeros((SUB, LRU_W), f32)
        hcar[...] = jnp.zeros((1, LRU_W), f32)

    ubuf[SUB:SUB + TT, :] = u_ref[...]
    xc = cb_ref[...]
    for j in range(CONV_W):
        off = SUB - (CONV_W - 1) + j
        xc = xc + ubuf[off:off + TT, :] * cw_ref[j:j + 1, :]
    conv_ref[...] = ubuf[TT + SUB - (CONV_W - 1):TT + SUB, :]
    ubuf[0:SUB, :] = ubuf[TT:TT + SUB, :]

    a, b = _lru_gates(xc, wa_ref, wi_ref, ba_ref, bi_ref, ap_ref)
    abuf[...] = a
    bbuf[...] = b

    row = lax.broadcasted_iota(jnp.int32, (SUB, LRU_W), 0)

    def group(g, h_prev):
        off = pl.multiple_of(g * SUB, SUB)
        ag = abuf[pl.ds(off, SUB), :]
        bg = bbuf[pl.ds(off, SUB), :]
        for d in (1, 2, 4):
            keep = row >= d
            a_sh = pltpu.roll(ag, d, 0)
            b_sh = pltpu.roll(bg, d, 0)
            bg = jnp.where(keep, ag * b_sh + bg, bg)
            ag = jnp.where(keep, ag * a_sh, ag)
        hg = ag * h_prev + bg
        bbuf[pl.ds(off, SUB), :] = hg
        return hg[SUB - 1:SUB, :]

    h_last = lax.fori_loop(0, TT // SUB, group, hcar[...])
    hcar[...] = h_last
    hlast_ref[...] = h_last
    o_ref[...] = (bbuf[...] * jax.nn.gelu(gate_ref[...])).astype(bf16)


def _lru_prompt_call(l, u, gate, o_init, cw, cb, wa, wi, ba, bi, ap):
    tile = pl.BlockSpec((TT, LRU_W), lambda b, t: (b * (SEQ // TT) + t, 0))
    vec = lambda: pl.BlockSpec((None, 1, LRU_W), lambda b, t: (l, 0, 0))
    return pl.pallas_call(
        _lru_prompt_kernel,
        grid=(BATCH, SEQ // TT),
        in_specs=[tile, tile,
                  pl.BlockSpec((None, CONV_W, LRU_W), lambda b, t: (l, 0, 0)), vec(),
                  pl.BlockSpec((None, 2, LRU_W // 2, LRU_W // 2), lambda b, t: (l, 0, 0, 0)),
                  pl.BlockSpec((None, 2, LRU_W // 2, LRU_W // 2), lambda b, t: (l, 0, 0, 0)),
                  vec(), vec(), vec(),
                  pl.BlockSpec(memory_space=pl.ANY)],
        out_specs=[tile,
                   pl.BlockSpec((None, CONV_W - 1, LRU_W), lambda b, t: (b, 0, 0)),
                   pl.BlockSpec((None, 1, LRU_W), lambda b, t: (b, 0, 0))],
        out_shape=[jax.ShapeDtypeStruct((N_PAD, LRU_W), bf16),
                   jax.ShapeDtypeStruct((BATCH, CONV_W - 1, LRU_W), f32),
                   jax.ShapeDtypeStruct((BATCH, 1, LRU_W), f32)],
        scratch_shapes=[pltpu.VMEM((TT + SUB, LRU_W), f32), pltpu.VMEM((TT, LRU_W), f32),
                        pltpu.VMEM((TT, LRU_W), f32), pltpu.VMEM((1, LRU_W), f32)],
        input_output_aliases={9: 0},
        compiler_params=pltpu.CompilerParams(dimension_semantics=("arbitrary", "arbitrary"),
                                             vmem_limit_bytes=VMEM_LIMIT),
        name=f"rglru_prompt_{l}",
    )(u, gate, cw, cb, wa, wi, ba, bi, ap, o_init)


def _lru_decode_kernel(u_ref, gate_ref, sc_ref, h0_ref, cw_ref, cb_ref, wa_ref, wi_ref,
                       ba_ref, bi_ref, ap_ref, o_ref, conv_ref, h_ref):
    u = u_ref[...]
    xc = cb_ref[...]
    for j in range(CONV_W - 1):
        xc = xc + sc_ref[j] * cw_ref[j:j + 1, :]
    xc = xc + u * cw_ref[CONV_W - 1:CONV_W, :]
    a, b = _lru_gates(xc, wa_ref, wi_ref, ba_ref, bi_ref, ap_ref)
    h = a * h0_ref[...] + b
    h_ref[...] = h
    for j in range(CONV_W - 2):
        conv_ref[j] = sc_ref[j + 1]
    conv_ref[CONV_W - 2] = u
    o_ref[...] = (h * jax.nn.gelu(gate_ref[...])).astype(bf16)


def _lru_decode_call(l, u, gate, sc_t, h0, cw, cb, wa, wi, ba, bi, ap):
    rows = pl.BlockSpec((DEC_BATCH, LRU_W), lambda i: (N_PROMPT // DEC_BATCH, 0))
    vec = lambda: pl.BlockSpec((None, 1, LRU_W), lambda i: (l, 0, 0))
    return pl.pallas_call(
        _lru_decode_kernel,
        grid=(1,),
        in_specs=[rows, rows,
                  pl.BlockSpec((None, CONV_W - 1, DEC_BATCH, LRU_W), lambda i: (l, 0, 0, 0)),
                  pl.BlockSpec((None, DEC_BATCH, LRU_W), lambda i: (l, 0, 0)),
                  pl.BlockSpec((None, CONV_W, LRU_W), lambda i: (l, 0, 0)), vec(),
                  pl.BlockSpec((None, 2, LRU_W // 2, LRU_W // 2), lambda i: (l, 0, 0, 0)),
                  pl.BlockSpec((None, 2, LRU_W // 2, LRU_W // 2), lambda i: (l, 0, 0, 0)),
                  vec(), vec(), vec()],
        out_specs=[pl.BlockSpec((DEC_BATCH, LRU_W), lambda i: (0, 0)),
                   pl.BlockSpec((CONV_W - 1, DEC_BATCH, LRU_W), lambda i: (0, 0, 0)),
                   pl.BlockSpec((DEC_BATCH, LRU_W), lambda i: (0, 0))],
        out_shape=[jax.ShapeDtypeStruct((DEC_BATCH, LRU_W), bf16),
                   jax.ShapeDtypeStruct((CONV_W - 1, DEC_BATCH, LRU_W), f32),
                   jax.ShapeDtypeStruct((DEC_BATCH, LRU_W), f32)],
        name=f"rglru_decode_{l}",
    )(u, gate, sc_t, h0, cw, cb, wa, wi, ba, bi, ap)


def _subln(o, gs, lam_init):
    ms = jnp.mean(o * o, axis=-1, keepdims=True)
    return (o * lax.rsqrt(ms + EPS) * gs) * (1.0 - lam_init)


def _attn_prompt_kernel(qt_ref, k_ref, vt_ref, wl_ref, gsb_ref, oinit_ref, o_ref,
                        m1, l1, a1, m2, l2, a2, *, lam_init):
    del oinit_ref
    qi = pl.program_id(2)
    q = qt_ref[...].astype(f32)
    comp_row = lax.broadcasted_iota(jnp.int32, (V_HD, TQ), 0)
    q1 = (jnp.where(comp_row < QK_HD, q, 0.0) * QK_SCALE).astype(bf16)
    q2 = (jnp.where(comp_row >= QK_HD, q, 0.0) * QK_SCALE).astype(bf16)
    for m_ref, l_ref, a_ref in ((m1, l1, a1), (m2, l2, a2)):
        m_ref[...] = jnp.full((1, TQ), NEG, f32)
        l_ref[...] = jnp.zeros((1, TQ), f32)
        a_ref[...] = jnp.zeros((V_HD, TQ), f32)

    def step(j, masked):
        off = pl.multiple_of(j * TQ, TQ)
        kb = k_ref[pl.ds(off, TQ), :]
        vt = vt_ref[:, pl.ds(off, TQ)]
        for qq, m_ref, l_ref, a_ref in ((q1, m1, l1, a1), (q2, m2, l2, a2)):
            s = jnp.dot(kb, qq, preferred_element_type=f32)
            if masked:
                key = lax.broadcasted_iota(jnp.int32, (TQ, TQ), 0)
                qry = lax.broadcasted_iota(jnp.int32, (TQ, TQ), 1)
                s = jnp.where(key <= qry, s, NEG)
            m_old = m_ref[...]
            m_new = jnp.maximum(m_old, jnp.max(s, axis=0, keepdims=True))
            alpha = jnp.exp(m_old - m_new)
            p = jnp.exp(s - m_new)
            l_ref[...] = alpha * l_ref[...] + jnp.sum(p, axis=0, keepdims=True)
            a_ref[...] = alpha * a_ref[...] + jnp.dot(vt, p.astype(bf16), preferred_element_type=f32)
            m_ref[...] = m_new

    def body(j, carry):
        step(j, False)
        return carry

    lax.fori_loop(0, qi, body, 0)
    step(qi, True)
    lam = _lam(wl_ref, lam_init)
    o = a1[...] * (1.0 / l1[...]) - lam * (a2[...] * (1.0 / l2[...]))
    ms = jnp.mean(o * o, axis=0, keepdims=True)
    y = (o * lax.rsqrt(ms + EPS) * gsb_ref[...]) * (1.0 - lam_init)
    o_ref[...] = y.astype(bf16)


def _attn_prompt_call(l, qt, kb, vt, o_init, w_lambda, gsb):
    lam_init = 0.8 - 0.6 * math.exp(-0.3 * l)
    nq = SEQ // TQ
    qo_spec = pl.BlockSpec((V_HD, TQ), lambda b, h, i: (h, b * nq + i))
    return pl.pallas_call(
        functools.partial(_attn_prompt_kernel, lam_init=lam_init),
        grid=(BATCH, N_HEADS, nq),
        in_specs=[qo_spec,
                  pl.BlockSpec((SEQ, V_HD), lambda b, h, i: (b, h)),
                  pl.BlockSpec((V_HD, SEQ), lambda b, h, i: (h, b)),
                  pl.BlockSpec((None, 4, QK_HD), lambda b, h, i: (l, 0, 0)),
                  pl.BlockSpec((None, V_HD, TQ), lambda b, h, i: (l, 0, 0)),
                  pl.BlockSpec(memory_space=pl.ANY)],
        out_specs=qo_spec,
        out_shape=jax.ShapeDtypeStruct((ATT_W, N_PAD), bf16),
        scratch_shapes=[pltpu.VMEM((1, TQ), f32), pltpu.VMEM((1, TQ), f32), pltpu.VMEM((V_HD, TQ), f32),
                        pltpu.VMEM((1, TQ), f32), pltpu.VMEM((1, TQ), f32), pltpu.VMEM((V_HD, TQ), f32)],
        input_output_aliases={5: 0},
        compiler_params=pltpu.CompilerParams(
            dimension_semantics=("arbitrary", "arbitrary", "arbitrary"),
            vmem_limit_bytes=VMEM_LIMIT),
        name=f"attn_prompt_{l}",
    )(qt, kb, vt, w_lambda, gsb, o_init)


def _attn_decode_kernel(pt_ref, q_ref, kn_ref, vn_ref, wl_ref, gs_ref, *rest, lam_init):
    del pt_ref
    k_refs = rest[:N_PAGES]
    v_refs = rest[N_PAGES:2 * N_PAGES]
    o_ref, s_buf, e_buf = rest[2 * N_PAGES:]
    nc = 2 * N_HEADS
    nt = (((1,), (1,)), ((), ()))
    row8 = lax.broadcasted_iota(jnp.int32, (nc, V_HD), 0)
    lane8 = lax.broadcasted_iota(jnp.int32, (nc, V_HD), 1)
    q8 = jnp.where(jnp.right_shift(lane8, 6) == jnp.bitwise_and(row8, 1), q_ref[...].astype(f32), 0.0)
    qn = jnp.concatenate([(q8 * QK_SCALE).astype(bf16), jnp.zeros((V_HD - nc, V_HD), bf16)], axis=0)

    rowp = lax.broadcasted_iota(jnp.int32, (PAGE_ROWS, V_HD), 0)
    colp = lax.broadcasted_iota(jnp.int32, (PAGE_ROWS, V_HD), 1)
    own = jnp.bitwise_and(rowp, N_HEADS - 1) == jnp.right_shift(colp, 1)
    own8 = jnp.where(row8 < N_HEADS, row8, -1) == jnp.right_shift(lane8, 1)

    m = jnp.full((1, V_HD), NEG, f32)
    for p in range(N_PAGES):
        s = lax.dot_general(k_refs[p][...].astype(bf16), qn, nt, preferred_element_type=f32)
        s = jnp.where(own, s, NEG)
        s_buf[p] = s
        m = jnp.maximum(m, jnp.max(s, axis=0, keepdims=True))
    s_new = lax.dot_general(kn_ref[...].astype(bf16), qn, nt, preferred_element_type=f32)
    s_new = jnp.where(own8, s_new, NEG)
    m = jnp.maximum(m, jnp.max(s_new, axis=0, keepdims=True))

    e_new = jnp.exp(s_new - m)
    denom = jnp.sum(e_new, axis=0, keepdims=True)
    for p in range(N_PAGES):
        e = jnp.exp(s_buf[p] - m)
        denom = denom + jnp.sum(e, axis=0, keepdims=True)
        e_buf[p] = e.astype(bf16)

    lam = _lam(wl_ref, lam_init)
    col1 = lax.broadcasted_iota(jnp.int32, (1, V_HD), 1)
    sign = jnp.where(jnp.bitwise_and(col1, 1) == 0, 1.0, -lam)
    coef = jnp.where(col1 < nc, sign, 0.0) / denom
    cmat = jnp.broadcast_to(coef, (V_HD, V_HD)).astype(bf16)

    def weighted(e, v):
        w = lax.dot_general(e, cmat, nt, preferred_element_type=f32)
        return w * v

    acc = weighted(e_new.astype(bf16), vn_ref[...])
    for p in range(N_PAGES):
        wv = weighted(e_buf[p], v_refs[p][...])
        acc = acc + jnp.sum(wv.reshape(PAGE_ROWS // SUB, SUB, V_HD), axis=0)
    o = acc[:N_HEADS, :] + acc[N_HEADS:, :]
    o_ref[...] = _subln(o, gs_ref[...], lam_init)


def _attn_decode_call(l, page_table, q_rep, kn8, vn8, cache_k, cache_v, w_lambda, g_subln3):
    lam_init = 0.8 - 0.6 * math.exp(-0.3 * l)
    row_spec = pl.BlockSpec((None, 2 * N_HEADS, V_HD), lambda b, pt: (b, 0, 0))

    def page_spec(p):
        return pl.BlockSpec((None, None, PAGE_ROWS, V_HD), lambda b, pt: (l, pt[b, p], 0, 0))

    grid_spec = pltpu.PrefetchScalarGridSpec(
        num_scalar_prefetch=1,
        grid=(DEC_BATCH,),
        in_specs=[row_spec, row_spec, row_spec,
                  pl.BlockSpec((None, 4, QK_HD), lambda b, pt: (l, 0, 0)),
                  pl.BlockSpec((None, 1, V_HD), lambda b, pt: (l, 0, 0))]
                 + [page_spec(p) for p in range(N_PAGES)] * 2,
        out_specs=pl.BlockSpec((None, N_HEADS, V_HD), lambda b, pt: (b, 0, 0)),
        scratch_shapes=[pltpu.VMEM((N_PAGES, PAGE_ROWS, V_HD), f32),
                        pltpu.VMEM((N_PAGES, PAGE_ROWS, V_HD), bf16)],
    )
    return pl.pallas_call(
        functools.partial(_attn_decode_kernel, lam_init=lam_init),
        grid_spec=grid_spec,
        out_shape=jax.ShapeDtypeStruct((DEC_BATCH, N_HEADS, V_HD), f32),
        compiler_params=pltpu.CompilerParams(dimension_semantics=("arbitrary",),
                                             vmem_limit_bytes=VMEM_LIMIT),
        name=f"attn_decode_{l}",
    )(page_table, q_rep, kn8, vn8, w_lambda, g_subln3,
      *([cache_k] * N_PAGES), *([cache_v] * N_PAGES))


def _block_diag(w):
    per = LRU_BLOCKS // 2
    w = w.reshape(DEPTH, 2, per, LRU_BS, LRU_BS)
    eye = jnp.eye(per, dtype=w.dtype)
    return jnp.einsum('dgnij,nm->dgnimj', w, eye).reshape(DEPTH, 2, per * LRU_BS, per * LRU_BS)


@jax.jit
def kernel(x_prompt, x_sample, cache_k, cache_v, page_table, state_conv, state_h, p_prompt, p_sample,
           w_in, w_out, conv_w, conv_b, w_a, b_a, w_i, b_i, a_param, w_lambda, g_subln,
           w_ffn_gate, w_ffn_up, w_ffn_down, w_pe, w_pg, g_norm, g_final):
    n_dec = DEC_BATCH
    pad_rows = N_PAD - N_PROMPT - n_dec
    x = jnp.concatenate([x_prompt.reshape(N_PROMPT, D_MODEL), x_sample.reshape(n_dec, D_MODEL),
                         jnp.zeros((pad_rows, D_MODEL), f32)], axis=0)
    p_all = jnp.concatenate([p_prompt.reshape(DEPTH, N_PROMPT, P_DIM), p_sample.reshape(DEPTH, n_dec, P_DIM),
                             jnp.zeros((DEPTH, pad_rows, P_DIM), f32)], axis=1)

    wg_b = w_ffn_gate.astype(bf16)
    wu_b = w_ffn_up.astype(bf16)
    wd_b = w_ffn_down.astype(bf16)
    win_b = w_in.astype(bf16)
    wo_b = w_out.astype(bf16)
    wpg_b = w_pg.astype(bf16)
    wpe_b = w_pe.astype(bf16)
    wa_b = _block_diag(w_a).astype(bf16)
    wi_b = _block_diag(w_i).astype(bf16)
    g_norm3 = g_norm.reshape(DEPTH, 4, 1, D_MODEL)
    g_final2 = g_final.reshape(1, D_MODEL)
    g_subln3 = g_subln.reshape(DEPTH, 1, V_HD)
    cb3 = conv_b.reshape(DEPTH, 1, LRU_W)
    ba3 = b_a.reshape(DEPTH, 1, LRU_W)
    bi3 = b_i.reshape(DEPTH, 1, LRU_W)
    ap3 = a_param.reshape(DEPTH, 1, LRU_W)
    sc_t = state_conv.transpose(0, 2, 1, 3)
    ck = cache_k.reshape(DEPTH, -1, PAGE_ROWS, V_HD)
    cv = cache_v.reshape(DEPTH, -1, PAGE_ROWS, V_HD)
    gsb = jnp.broadcast_to(g_subln[:, :, None], (DEPTH, V_HD, TQ))
    tabs = _rope_tables()

    ks_s, vs_s, convs_p, hs_p, convs_s, hs_s = [], [], [], [], [], []
    kfin = vfin = None
    dec = slice(N_PROMPT, N_PROMPT + n_dec)
    dec_rows = n_dec * N_HEADS
    for l in range(DEPTH):
        x, u, gate, qs, qt, kb, vt, kfin, vfin, ksn, vsn = _x_call(
            l, x, g_norm3, wg_b, wu_b, wd_b, win_b, tabs, kfin, vfin)
        k_new = ksn[:dec_rows].reshape(n_dec, N_HEADS, V_HD)
        v_new = vsn[:dec_rows].reshape(n_dec, N_HEADS, V_HD)
        ks_s.append(k_new)
        vs_s.append(v_new)

        o_lru, conv_p, h_p = _lru_prompt_call(l, u, gate, jnp.zeros((N_PAD, LRU_W), bf16),
                                              conv_w, cb3, wa_b, wi_b, ba3, bi3, ap3)
        o_lru_s, conv_s, h_s = _lru_decode_call(l, u, gate, sc_t, state_h,
                                                conv_w, cb3, wa_b, wi_b, ba3, bi3, ap3)
        o_lru = lax.dynamic_update_slice(o_lru, o_lru_s, (N_PROMPT, 0))
        convs_p.append(conv_p)
        hs_p.append(h_p.reshape(BATCH, LRU_W))
        convs_s.append(conv_s.transpose(1, 0, 2))
        hs_s.append(h_s)

        o_attn_t = _attn_prompt_call(l, qt, kb, vt, jnp.zeros((ATT_W, N_PAD), bf16), w_lambda, gsb)
        q_rep = jnp.repeat(qs[:n_dec].reshape(n_dec, N_HEADS, V_HD), 2, axis=1)
        pad8 = ((0, 0), (0, 2 * N_HEADS - N_HEADS), (0, 0))
        o_attn_s = _attn_decode_call(l, page_table, q_rep, jnp.pad(k_new, pad8), jnp.pad(v_new, pad8),
                                     ck, cv, w_lambda, g_subln3)
        o_attn_t = lax.dynamic_update_slice(
            o_attn_t, o_attn_s.reshape(n_dec, ATT_W).astype(bf16).T, (0, N_PROMPT))

        x = _y_call(l, x, o_lru, o_attn_t, p_all, wo_b, g_norm3, wg_b, wu_b, wd_b, wpg_b, wpe_b, g_final2)
    y = x

    y_prompt = y[:N_PROMPT].reshape(BATCH, SEQ, D_MODEL)
    y_sample = y[dec].reshape(n_dec, 1, D_MODEL)
    k_prompt = kfin.reshape(DEPTH, BATCH, SEQ, N_HEADS, V_HD)
    v_prompt = vfin.reshape(DEPTH, BATCH, SEQ, N_HEADS, V_HD)
    k_sample = jnp.stack(ks_s).reshape(DEPTH, n_dec, 1, N_HEADS, V_HD)
    v_sample = jnp.stack(vs_s).reshape(DEPTH, n_dec, 1, N_HEADS, V_HD)
    return (y_prompt, y_sample, k_prompt, v_prompt, jnp.stack(convs_p), jnp.stack(hs_p),
            k_sample, v_sample, jnp.stack(convs_s), jnp.stack(hs_s))
```

```python
import functools
import math

import jax
import jax.numpy as jnp
from jax import lax
from jax.experimental import pallas as pl
from jax.experimental.pallas import tpu as pltpu

f32 = jnp.float32
bf16 = jnp.bfloat16

D_MODEL = 1024
BATCH = 4
SEQ = 4096
DEPTH = 4
DEC_BATCH = 128
PAST_LEN = 2048
PAGE_SIZE = 128
N_PAGES = PAST_LEN // PAGE_SIZE
N_HEADS = 4
PAGE_ROWS = PAGE_SIZE * N_HEADS
LRU_W = 512
LRU_BLOCKS = 8
LRU_BS = 64
CONV_W = 4
C_GATE = 8.0
ATT_W = 512
QK_HD = 64
V_HD = 128
ROT_DIM = 16
ROPE_THETA = 500000.0
D_FF = 2816
P_DIM = 256
EPS = 1e-6
NEG = -1e30
QK_SCALE = QK_HD ** -0.5
LOG2E = math.log2(math.e)

N_PROMPT = BATCH * SEQ
TM = 512
N_TILES_P = N_PROMPT // TM
N_TILES = N_TILES_P + 1
N_PAD = N_TILES * TM
SEQ_TILES = SEQ // TM
FF_CHUNKS = 2
FF_C = D_FF // FF_CHUNKS
TQ = 512
TT = 512
DEC_PER_STEP = 2
SUB = 8
VMEM_LIMIT = 56 * 1024 * 1024


def _const_spec(shape):
    nd = len(shape)
    return pl.BlockSpec(shape, lambda *_: (0,) * nd, pipeline_mode=pl.Buffered(1))


def _layer_spec(shape, lead):
    nl = len(lead)
    nd = len(shape)
    return pl.BlockSpec((None,) * nl + tuple(shape), lambda *_: tuple(lead) + (0,) * nd,
                        pipeline_mode=pl.Buffered(1))


def _rms(x, g):
    ms = jnp.mean(x * x, axis=-1, keepdims=True)
    return x * lax.rsqrt(ms + EPS) * g


def _swiglu(hb, wg_ref, wu_ref, wd_ref):
    acc = None
    for c in range(FF_CHUNKS):
        sl = slice(c * FF_C, (c + 1) * FF_C)
        g = jnp.dot(hb, wg_ref[:, sl], preferred_element_type=f32)
        u = jnp.dot(hb, wu_ref[:, sl], preferred_element_type=f32)
        a = (g * jax.nn.sigmoid(g) * u).astype(bf16)
        y = jnp.dot(a, wd_ref[sl, :], preferred_element_type=f32)
        acc = y if acc is None else acc + y
    return acc


def _lam(wl_ref, lam_init):
    wl = wl_ref[...]
    s01 = jnp.sum(wl[0:1, :] * wl[1:2, :], axis=-1, keepdims=True)
    s23 = jnp.sum(wl[2:3, :] * wl[3:4, :], axis=-1, keepdims=True)
    return jnp.exp(s01) - jnp.exp(s23) + lam_init


def _rope_table_kernel(freq_ref, c_ref, sa_ref, sb_ref):
    rows = c_ref.shape[0]
    row = lax.broadcasted_iota(jnp.int32, (rows, V_HD), 0)
    lane = lax.broadcasted_iota(jnp.int32, (rows, V_HD), 1)
    pos = jnp.where(row < SEQ, row, PAST_LEN).astype(f32)
    ang = pos * freq_ref[...]
    cos = jnp.cos(ang)
    sin = jnp.sin(ang)
    in_comp = lane % QK_HD
    half = ROT_DIM // 2
    c_ref[...] = jnp.where(in_comp < ROT_DIM, cos, 1.0)
    sa_ref[...] = jnp.where(in_comp < half, -sin, 0.0)
    sb_ref[...] = jnp.where((in_comp >= half) & (in_comp < ROT_DIM), sin, 0.0)


def _rope_tables():
    half = ROT_DIM // 2
    freqs = jnp.power(jnp.float32(ROPE_THETA), -jnp.arange(0, ROT_DIM, 2, dtype=f32) / ROT_DIM)
    lane = jnp.arange(V_HD)
    freq_lane = freqs[(lane % QK_HD) % half].reshape(1, V_HD)
    rows = SEQ + TM
    shp = jax.ShapeDtypeStruct((rows, V_HD), f32)
    return pl.pallas_call(
        _rope_table_kernel,
        out_shape=(shp, shp, shp),
        name="rope_tables",
    )(freq_lane)


def _store_heads_interleaved(dst_ref, t):
    for hd in range(N_HEADS):
        dst_ref[pl.ds(hd, TM, stride=N_HEADS), :] = t[:, hd * V_HD:(hd + 1) * V_HD]


def _x_kernel(x_ref, gn0_ref, gn1_ref, wg_ref, wu_ref, wd_ref, win_ref, c_ref, sa_ref, sb_ref, *rest):
    (xo_ref, u_ref, gate_ref, qs_ref, qt_ref, kb_ref, vt_ref,
     kfin_ref, vfin_ref, ksn_ref, vsn_ref) = rest[-11:]
    i = pl.program_id(0)
    x = x_ref[...]
    h = _rms(x, gn0_ref[...]).astype(bf16)
    x1 = x + 0.5 * _swiglu(h, wg_ref, wu_ref, wd_ref)
    xo_ref[...] = x1
    h1 = _rms(x1, gn1_ref[...]).astype(bf16)

    def proj(i):
        return jnp.dot(h1, win_ref[:, i * LRU_W:(i + 1) * LRU_W], preferred_element_type=f32)

    u_ref[...] = proj(0)
    gate_ref[...] = proj(1)
    cos = c_ref[...]
    sa = sa_ref[...]
    sb = sb_ref[...]

    def rope(t):
        outs = []
        for hd in range(N_HEADS):
            th = t[:, hd * V_HD:(hd + 1) * V_HD]
            up = pltpu.roll(th, V_HD - ROT_DIM // 2, 1)
            dn = pltpu.roll(th, ROT_DIM // 2, 1)
            outs.append(th * cos + up * sa + dn * sb)
        return jnp.concatenate(outs, axis=-1)

    q = rope(proj(2))
    qt_ref[...] = (q * (QK_SCALE * LOG2E)).T.astype(bf16)
    k = rope(proj(3))
    kb_ref[...] = k.astype(bf16)
    v = proj(4)
    vt_ref[...] = v.T.astype(bf16)

    @pl.when(i < N_TILES_P)
    def _():
        _store_heads_interleaved(kfin_ref, k)
        _store_heads_interleaved(vfin_ref, v)

    @pl.when(i == N_TILES_P)
    def _():
        qs_ref[...] = q.astype(bf16)
        _store_heads_interleaved(ksn_ref, k)
        _store_heads_interleaved(vsn_ref, v)


def _x_call(l, x, g_norm3, wg, wu, wd, win, tabs, kfin, vfin):
    tile = lambda w: pl.BlockSpec((TM, w), lambda i: (i, 0))
    tile_t = pl.BlockSpec((LRU_W, TM), lambda i: (0, i))
    tab_spec = pl.BlockSpec((TM, V_HD), lambda i: (jnp.where(i < N_TILES_P, i % SEQ_TILES, SEQ_TILES), 0))
    fin_spec = pl.BlockSpec((None, TM * N_HEADS, V_HD), lambda i: (l, jnp.minimum(i, N_TILES_P - 1), 0))
    dec_spec = lambda w: pl.BlockSpec((TM * N_HEADS, w), lambda i: (0, 0))
    act = lambda dt: jax.ShapeDtypeStruct((N_PAD, LRU_W), dt)
    act_t = jax.ShapeDtypeStruct((LRU_W, N_PAD), bf16)
    fin = jax.ShapeDtypeStruct((DEPTH, N_PROMPT * N_HEADS, V_HD), f32)
    dec = jax.ShapeDtypeStruct((TM * N_HEADS, V_HD), f32)
    in_specs = [tile(D_MODEL),
                _layer_spec((1, D_MODEL), (l, 0)), _layer_spec((1, D_MODEL), (l, 1)),
                _layer_spec((D_MODEL, D_FF), (l, 0)), _layer_spec((D_MODEL, D_FF), (l, 0)),
                _layer_spec((D_FF, D_MODEL), (l, 0)), _layer_spec((D_MODEL, 5 * LRU_W), (l,)),
                tab_spec, tab_spec, tab_spec]
    args = [x, g_norm3, g_norm3, wg, wu, wd, win, *tabs]
    aliases = {}
    if kfin is not None:
        aliases = {len(args): 7, len(args) + 1: 8}
        in_specs += [pl.BlockSpec(memory_space=pl.ANY)] * 2
        args += [kfin, vfin]
    return pl.pallas_call(
        _x_kernel,
        grid=(N_TILES,),
        in_specs=in_specs,
        out_specs=[tile(D_MODEL), tile(LRU_W), tile(LRU_W),
                   pl.BlockSpec((TM, ATT_W), lambda i: (0, 0)), tile_t, tile(ATT_W), tile_t,
                   fin_spec, fin_spec, dec_spec(V_HD), dec_spec(V_HD)],
        out_shape=[jax.ShapeDtypeStruct((N_PAD, D_MODEL), f32), act(f32), act(f32),
                   jax.ShapeDtypeStruct((TM, ATT_W), bf16), act_t, act(bf16), act_t,
                   fin, fin, dec, dec],
        input_output_aliases=aliases,
        compiler_params=pltpu.CompilerParams(dimension_semantics=("arbitrary",),
                                             vmem_limit_bytes=VMEM_LIMIT),
        name=f"ffn1_inproj_{l}",
    )(*args)


def _y_kernel(x_ref, ol_ref, oa_ref, p_ref, wo_ref, gn2_ref, wg_ref, wu_ref, wd_ref,
              gn3_ref, wpg_ref, wpe_ref, *rest, final):
    x = x_ref[...]
    x2 = (x + jnp.dot(ol_ref[...], wo_ref[:LRU_W, :], preferred_element_type=f32)
          + lax.dot_general(oa_ref[...], wo_ref[LRU_W:, :], (((0,), (0,)), ((), ())),
                            preferred_element_type=f32))
    h = _rms(x2, gn2_ref[...]).astype(bf16)
    x3 = x2 + 0.5 * _swiglu(h, wg_ref, wu_ref, wd_ref)
    h3 = _rms(x3, gn3_ref[...]).astype(bf16)
    gate = jax.nn.sigmoid(jnp.dot(h3, wpg_ref[...], preferred_element_type=f32))
    pe = jnp.dot(p_ref[...].astype(bf16), wpe_ref[...], preferred_element_type=f32)
    x4 = x3 + gate * pe
    if final:
        gf_ref, yp_ref, ys_ref = rest
        y = _rms(x4, gf_ref[...])
        i = pl.program_id(0)

        @pl.when(i < N_TILES_P)
        def _():
            yp_ref[...] = y

        @pl.when(i == N_TILES_P)
        def _():
            ys_ref[...] = y
    else:
        (xo_ref,) = rest
        xo_ref[...] = x4


def _y_call(l, x, o_lru, o_attn, p_all, wo, g_norm3, wg, wu, wd, wpg, wpe, g_final):
    final = l == DEPTH - 1
    tile = lambda w: pl.BlockSpec((TM, w), lambda i: (i, 0))
    in_specs = [tile(D_MODEL), tile(LRU_W), pl.BlockSpec((ATT_W, TM), lambda i: (0, i)),
                pl.BlockSpec((None, TM, P_DIM), lambda i: (l, i, 0)),
                _layer_spec((D_MODEL, D_MODEL), (l,)),
                _layer_spec((1, D_MODEL), (l, 2)),
                _layer_spec((D_MODEL, D_FF), (l, 1)), _layer_spec((D_MODEL, D_FF), (l, 1)),
                _layer_spec((D_FF, D_MODEL), (l, 1)),
                _layer_spec((1, D_MODEL), (l, 3)),
                _layer_spec((D_MODEL, D_MODEL), (l,)), _layer_spec((P_DIM, D_MODEL), (l,))]
    args = [x, o_lru, o_attn, p_all, wo, g_norm3, wg, wu, wd, g_norm3, wpg, wpe]
    out_specs = tile(D_MODEL)
    out_shape = jax.ShapeDtypeStruct((N_PAD, D_MODEL), f32)
    if final:
        in_specs.append(_const_spec((1, D_MODEL)))
        args.append(g_final)
        out_specs = [pl.BlockSpec((TM, D_MODEL), lambda i: (jnp.minimum(i, N_TILES_P - 1), 0)),
                     pl.BlockSpec((TM, D_MODEL), lambda i: (0, 0))]
        out_shape = [jax.ShapeDtypeStruct((N_PROMPT, D_MODEL), f32),
                     jax.ShapeDtypeStruct((TM, D_MODEL), f32)]
    return pl.pallas_call(
        functools.partial(_y_kernel, final=final),
        grid=(N_TILES,),
        in_specs=in_specs,
        out_specs=out_specs,
        out_shape=out_shape,
        compiler_params=pltpu.CompilerParams(dimension_semantics=("arbitrary",),
                                             vmem_limit_bytes=VMEM_LIMIT),
        name=f"outproj_ffn2_embed_{l}",
    )(*args)


def _softplus(x):
    return jnp.maximum(x, 0.0) + jnp.log1p(jnp.exp(-jnp.abs(x)))


def _lru_gates(xc, wa_ref, wi_ref, ba_ref, bi_ref, ap_ref):
    xb = xc.astype(bf16)
    half = LRU_W // 2

    def bd(w_ref):
        lo = jnp.dot(xb[:, :half], w_ref[0], preferred_element_type=f32)
        hi = jnp.dot(xb[:, half:], w_ref[1], preferred_element_type=f32)
        return jnp.concatenate([lo, hi], axis=-1)

    r = jax.nn.sigmoid(bd(wa_ref) + ba_ref[...])
    i = jax.nn.sigmoid(bd(wi_ref) + bi_ref[...])
    log_a = -C_GATE * r * _softplus(-ap_ref[...])
    a = jnp.exp(log_a)
    b = jnp.sqrt(-jnp.tanh(log_a) * (a * a + 1.0)) * (i * xc)
    return a, b


def _lru_prompt_kernel(u_ref, gate_ref, cw_ref, cb_ref, wa_ref, wi_ref, ba_ref, bi_ref, ap_ref,
                       oinit_ref, o_ref, conv_ref, hlast_ref, ubuf, abuf, bbuf, hcar):
    del oinit_ref
    t = pl.program_id(1)

    @pl.when(t == 0)
    def _():
        ubuf[0:SUB, :] = jnp.zeros((SUB, LRU_W), f32)
        hcar[...] = jnp.zeros((1, LRU_W), f32)

    ubuf[SUB:SUB + TT, :] = u_ref[...]
    xc = cb_ref[...]
    for j in range(CONV_W):
        off = SUB - (CONV_W - 1) + j
        xc = xc + ubuf[off:off + TT, :] * cw_ref[j:j + 1, :]
    conv_ref[...] = ubuf[TT + SUB - (CONV_W - 1):TT + SUB, :]
    ubuf[0:SUB, :] = ubuf[TT:TT + SUB, :]

    a, b = _lru_gates(xc, wa_ref, wi_ref, ba_ref, bi_ref, ap_ref)
    abuf[...] = a
    bbuf[...] = b

    row = lax.broadcasted_iota(jnp.int32, (SUB, LRU_W), 0)

    def group(g, h_prev):
        off = pl.multiple_of(g * SUB, SUB)
        ag = abuf[pl.ds(off, SUB), :]
        bg = bbuf[pl.ds(off, SUB), :]
        for d in (1, 2, 4):
            keep = row >= d
            a_sh = pltpu.roll(ag, d, 0)
            b_sh = pltpu.roll(bg, d, 0)
            bg = jnp.where(keep, ag * b_sh + bg, bg)
            ag = jnp.where(keep, ag * a_sh, ag)
        hg = ag * h_prev + bg
        bbuf[pl.ds(off, SUB), :] = hg
        return hg[SUB - 1:SUB, :]

    h_last = lax.fori_loop(0, TT // SUB, group, hcar[...])
    hcar[...] = h_last
    hlast_ref[...] = h_last
    o_ref[...] = (bbuf[...] * jax.nn.gelu(gate_ref[...])).astype(bf16)


def _lru_prompt_call(l, u, gate, o_init, cw, cb, wa, wi, ba, bi, ap):
    tile = pl.BlockSpec((TT, LRU_W), lambda b, t: (b * (SEQ // TT) + t, 0))
    vec = lambda: pl.BlockSpec((None, 1, LRU_W), lambda b, t: (l, 0, 0))
    return pl.pallas_call(
        _lru_prompt_kernel,
        grid=(BATCH, SEQ // TT),
        in_specs=[tile, tile,
                  pl.BlockSpec((None, CONV_W, LRU_W), lambda b, t: (l, 0, 0)), vec(),
                  pl.BlockSpec((None, 2, LRU_W // 2, LRU_W // 2), lambda b, t: (l, 0, 0, 0)),
                  pl.BlockSpec((None, 2, LRU_W // 2, LRU_W // 2), lambda b, t: (l, 0, 0, 0)),
                  vec(), vec(), vec(),
                  pl.BlockSpec(memory_space=pl.ANY)],
        out_specs=[tile,
                   pl.BlockSpec((None, CONV_W - 1, LRU_W), lambda b, t: (b, 0, 0)),
                   pl.BlockSpec((None, 1, LRU_W), lambda b, t: (b, 0, 0))],
        out_shape=[jax.ShapeDtypeStruct((N_PAD, LRU_W), bf16),
                   jax.ShapeDtypeStruct((BATCH, CONV_W - 1, LRU_W), f32),
                   jax.ShapeDtypeStruct((BATCH, 1, LRU_W), f32)],
        scratch_shapes=[pltpu.VMEM((TT + SUB, LRU_W), f32), pltpu.VMEM((TT, LRU_W), f32),
                        pltpu.VMEM((TT, LRU_W), f32), pltpu.VMEM((1, LRU_W), f32)],
        input_output_aliases={9: 0},
        compiler_params=pltpu.CompilerParams(dimension_semantics=("arbitrary", "arbitrary"),
                                             vmem_limit_bytes=VMEM_LIMIT),
        name=f"rglru_prompt_{l}",
    )(u, gate, cw, cb, wa, wi, ba, bi, ap, o_init)


def _lru_decode_kernel(u_ref, gate_ref, sc_ref, h0_ref, cw_ref, cb_ref, wa_ref, wi_ref,
                       ba_ref, bi_ref, ap_ref, o_ref, conv_ref, h_ref):
    u = u_ref[...]
    xc = cb_ref[...]
    for j in range(CONV_W - 1):
        xc = xc + sc_ref[j] * cw_ref[j:j + 1, :]
    xc = xc + u * cw_ref[CONV_W - 1:CONV_W, :]
    a, b = _lru_gates(xc, wa_ref, wi_ref, ba_ref, bi_ref, ap_ref)
    h = a * h0_ref[...] + b
    h_ref[...] = h
    for j in range(CONV_W - 2):
        conv_ref[j] = sc_ref[j + 1]
    conv_ref[CONV_W - 2] = u
    o_ref[...] = (h * jax.nn.gelu(gate_ref[...])).astype(bf16)


def _lru_decode_call(l, u, gate, sc_t, h0, cw, cb, wa, wi, ba, bi, ap):
    rows = pl.BlockSpec((DEC_BATCH, LRU_W), lambda i: (N_PROMPT // DEC_BATCH, 0))
    vec = lambda: pl.BlockSpec((None, 1, LRU_W), lambda i: (l, 0, 0))
    return pl.pallas_call(
        _lru_decode_kernel,
        grid=(1,),
        in_specs=[rows, rows,
                  pl.BlockSpec((None, CONV_W - 1, DEC_BATCH, LRU_W), lambda i: (l, 0, 0, 0)),
                  pl.BlockSpec((None, DEC_BATCH, LRU_W), lambda i: (l, 0, 0)),
                  pl.BlockSpec((None, CONV_W, LRU_W), lambda i: (l, 0, 0)), vec(),
                  pl.BlockSpec((None, 2, LRU_W // 2, LRU_W // 2), lambda i: (l, 0, 0, 0)),
                  pl.BlockSpec((None, 2, LRU_W // 2, LRU_W // 2), lambda i: (l, 0, 0, 0)),
                  vec(), vec(), vec()],
        out_specs=[pl.BlockSpec((DEC_BATCH, LRU_W), lambda i: (0, 0)),
                   pl.BlockSpec((CONV_W - 1, DEC_BATCH, LRU_W), lambda i: (0, 0, 0)),
                   pl.BlockSpec((DEC_BATCH, LRU_W), lambda i: (0, 0))],
        out_shape=[jax.ShapeDtypeStruct((DEC_BATCH, LRU_W), bf16),
                   jax.ShapeDtypeStruct((CONV_W - 1, DEC_BATCH, LRU_W), f32),
                   jax.ShapeDtypeStruct((DEC_BATCH, LRU_W), f32)],
        name=f"rglru_decode_{l}",
    )(u, gate, sc_t, h0, cw, cb, wa, wi, ba, bi, ap)


def _subln(o, gs, lam_init):
    ms = jnp.mean(o * o, axis=-1, keepdims=True)
    return (o * lax.rsqrt(ms + EPS) * gs) * (1.0 - lam_init)


def _attn_prompt_kernel(qt_ref, k_ref, vt_ref, wl_ref, gsb_ref, oinit_ref, o_ref,
                        s_buf0, s_buf1, m1, a1, m2, a2, *, lam_init):
    del oinit_ref
    qi = pl.program_id(2)
    q = qt_ref[...].astype(f32)
    comp_row = lax.broadcasted_iota(jnp.int32, (V_HD, TQ), 0)
    qs = (jnp.where(comp_row < QK_HD, q, 0.0).astype(bf16),
          jnp.where(comp_row >= QK_HD, q, 0.0).astype(bf16))
    stats = ((m1, a1), (m2, a2))
    for m_ref, a_ref in stats:
        m_ref[...] = jnp.full((1, TQ), NEG, f32)
        a_ref[...] = jnp.zeros((V_HD + SUB, TQ), f32)
    ones = jnp.ones((SUB, TQ), bf16)

    def scores(off, s_ref, masked):
        kb = k_ref[pl.ds(pl.multiple_of(off, TQ), TQ), :]
        for c in range(2):
            s = jnp.dot(kb, qs[c], preferred_element_type=f32)
            if masked:
                key = lax.broadcasted_iota(jnp.int32, (TQ, TQ), 0)
                qry = lax.broadcasted_iota(jnp.int32, (TQ, TQ), 1)
                s = jnp.where(key <= qry, s, NEG)
            s_ref[c] = s

    def consume(off, s_ref):
        vt = jnp.concatenate([vt_ref[:, pl.ds(pl.multiple_of(off, TQ), TQ)], ones], axis=0)
        for c, (m_ref, a_ref) in enumerate(stats):
            s = s_ref[c]
            m_old = m_ref[...]
            m_new = jnp.maximum(m_old, jnp.max(s, axis=0, keepdims=True))
            alpha = jnp.exp2(m_old - m_new)
            p = jnp.exp2(s - m_new).astype(bf16)
            a_ref[...] = alpha * a_ref[...] + jnp.dot(vt, p, preferred_element_type=f32)
            m_ref[...] = m_new

    diag = qi * TQ
    scores(diag, s_buf0, True)

    def pair(u, prev_off):
        off1 = 2 * u * TQ
        scores(off1, s_buf1, False)
        consume(prev_off, s_buf0)
        off2 = off1 + TQ
        scores(off2, s_buf0, False)
        consume(off1, s_buf1)
        return off2

    last_off = lax.fori_loop(0, qi // 2, pair, diag)

    @pl.when(qi % 2 == 1)
    def _():
        off = (qi - 1) * TQ
        scores(off, s_buf1, False)
        consume(last_off, s_buf0)
        consume(off, s_buf1)

    @pl.when(qi % 2 == 0)
    def _():
        consume(last_off, s_buf0)

    lam = _lam(wl_ref, lam_init)
    inv1 = 1.0 / a1[V_HD:V_HD + 1, :]
    inv2 = 1.0 / a2[V_HD:V_HD + 1, :]
    o = a1[:V_HD, :] * inv1 - lam * (a2[:V_HD, :] * inv2)
    ms = jnp.mean(o * o, axis=0, keepdims=True)
    y = (o * lax.rsqrt(ms + EPS) * gsb_ref[...]) * (1.0 - lam_init)
    o_ref[...] = y.astype(bf16)


def _attn_prompt_call(l, qt, kb, vt, o_init, w_lambda, gsb):
    lam_init = 0.8 - 0.6 * math.exp(-0.3 * l)
    nq = SEQ // TQ
    qo_spec = pl.BlockSpec((V_HD, TQ), lambda b, h, i: (h, b * nq + i))
    return pl.pallas_call(
        functools.partial(_attn_prompt_kernel, lam_init=lam_init),
        grid=(BATCH, N_HEADS, nq),
        in_specs=[qo_spec,
                  pl.BlockSpec((SEQ, V_HD), lambda b, h, i: (b, h)),
                  pl.BlockSpec((V_HD, SEQ), lambda b, h, i: (h, b)),
                  pl.BlockSpec((None, 4, QK_HD), lambda b, h, i: (l, 0, 0)),
                  pl.BlockSpec((None, V_HD, TQ), lambda b, h, i: (l, 0, 0)),
                  pl.BlockSpec(memory_space=pl.ANY)],
        out_specs=qo_spec,
        out_shape=jax.ShapeDtypeStruct((ATT_W, N_PAD), bf16),
        scratch_shapes=[pltpu.VMEM((2, TQ, TQ), f32), pltpu.VMEM((2, TQ, TQ), f32),
                        pltpu.VMEM((1, TQ), f32), pltpu.VMEM((V_HD + SUB, TQ), f32),
                        pltpu.VMEM((1, TQ), f32), pltpu.VMEM((V_HD + SUB, TQ), f32)],
        input_output_aliases={5: 0},
        compiler_params=pltpu.CompilerParams(
            dimension_semantics=("arbitrary", "arbitrary", "arbitrary"),
            vmem_limit_bytes=VMEM_LIMIT),
        name=f"attn_prompt_{l}",
    )(qt, kb, vt, w_lambda, gsb, o_init)


def _attn_decode_kernel(pt_ref, q_ref, kn_ref, vn_ref, wl_ref, gs_ref, *rest, lam_init):
    del pt_ref
    o_ref = rest[2 * DEC_PER_STEP * N_PAGES]
    lam = _lam(wl_ref, lam_init)
    for j in range(DEC_PER_STEP):
        k_refs = rest[j * N_PAGES:(j + 1) * N_PAGES]
        v_refs = rest[(DEC_PER_STEP + j) * N_PAGES:(DEC_PER_STEP + j + 1) * N_PAGES]
        o = _decode_token(q_ref[j].astype(f32), kn_ref[j], vn_ref[j], k_refs, v_refs, lam)
        o_ref[j] = _subln(o, gs_ref[...], lam_init)


def _decode_token(q, kn, vn, k_refs, v_refs, lam):
    nc = 2 * N_HEADS
    nt = (((1,), (1,)), ((), ()))
    row8 = lax.broadcasted_iota(jnp.int32, (nc, V_HD), 0)
    lane8 = lax.broadcasted_iota(jnp.int32, (nc, V_HD), 1)
    lane1 = lax.broadcasted_iota(jnp.int32, (1, V_HD), 1)
    q8 = jnp.where(jnp.right_shift(lane8, 6) == jnp.bitwise_and(row8, 1), q, 0.0)
    q8 = q8 * QK_SCALE
    q_rows = jnp.concatenate([q8] * (V_HD // nc), axis=0)
    q_rows = jnp.concatenate([q_rows] * N_PAGES, axis=1)
    rq = lax.broadcasted_iota(jnp.int32, (V_HD, N_PAGES * V_HD), 0)
    lq = lax.broadcasted_iota(jnp.int32, (V_HD, N_PAGES * V_HD), 1)
    qmat = jnp.where(jnp.right_shift(rq, 3) == jnp.right_shift(lq, 7), q_rows, 0.0).astype(bf16)

    half = N_PAGES // 2
    s = None
    for g in range(2):
        kcat = jnp.concatenate([k_refs[p][...].astype(bf16) for p in range(g * half, (g + 1) * half)],
                               axis=1)
        sg = lax.dot_general(kcat, qmat[:, g * half * V_HD:(g + 1) * half * V_HD], nt,
                             preferred_element_type=f32)
        s = sg if s is None else s + sg
    rowp = lax.broadcasted_iota(jnp.int32, (PAGE_ROWS, V_HD), 0)
    colp = lax.broadcasted_iota(jnp.int32, (PAGE_ROWS, V_HD), 1)
    own = jnp.bitwise_and(rowp, N_HEADS - 1) == jnp.right_shift(jnp.bitwise_and(colp, nc - 1), 1)
    s = jnp.where(own, s, NEG)
    s_new = lax.dot_general(kn.astype(bf16), qmat[:, :V_HD], nt, preferred_element_type=f32)
    own8 = (jnp.where(row8 < N_HEADS, row8, -1)
            == jnp.where(lane8 < nc, jnp.right_shift(lane8, 1), -2))
    s_new = jnp.where(own8, s_new, NEG)

    def over_pages(x, op):
        x = jnp.broadcast_to(x, (SUB, V_HD))
        for sh in (nc, 2 * nc, 4 * nc, 8 * nc):
            x = op(x, pltpu.roll(x, sh, 1))
        return x[0:1, :]

    m = over_pages(jnp.maximum(jnp.max(s, axis=0, keepdims=True),
                               jnp.max(s_new, axis=0, keepdims=True)), jnp.maximum)
    e = jnp.exp(s - m)
    e_new = jnp.exp(s_new - m)
    denom = over_pages(jnp.sum(e, axis=0, keepdims=True) + jnp.sum(e_new, axis=0, keepdims=True),
                       jnp.add)
    coef = jnp.where(jnp.bitwise_and(lane1, 1) == 0, 1.0, -lam) / denom
    eb = e.astype(bf16)

    def lane_weights(first_lane):
        sel = jnp.where(jnp.right_shift(lane1, 3) == first_lane // nc, coef, 0.0)
        return jnp.broadcast_to(sel, (V_HD, V_HD)).astype(bf16)

    w_new = lax.dot_general(e_new.astype(bf16), lane_weights(0), nt, preferred_element_type=f32)
    acc = w_new * vn
    for p in range(0, N_PAGES, 2):
        both = jnp.concatenate([lane_weights(nc * p), lane_weights(nc * (p + 1))], axis=0)
        w = lax.dot_general(eb, both, nt, preferred_element_type=f32)
        wv = w[:, :V_HD] * v_refs[p][...] + w[:, V_HD:] * v_refs[p + 1][...]
        acc = acc + jnp.sum(wv.reshape(PAGE_ROWS // SUB, SUB, V_HD), axis=0)
    return acc[:N_HEADS, :] + acc[N_HEADS:, :]


def _attn_decode_call(l, page_table, q_rep, kn8, vn8, cache_k, cache_v, w_lambda, g_subln3):
    lam_init = 0.8 - 0.6 * math.exp(-0.3 * l)
    row_spec = pl.BlockSpec((DEC_PER_STEP, 2 * N_HEADS, V_HD), lambda b, pt: (b, 0, 0))

    def page_spec(j, p):
        return pl.BlockSpec((None, None, PAGE_ROWS, V_HD),
                            lambda b, pt: (l, pt[b * DEC_PER_STEP + j, p], 0, 0))

    pages = [page_spec(j, p) for j in range(DEC_PER_STEP) for p in range(N_PAGES)]
    grid_spec = pltpu.PrefetchScalarGridSpec(
        num_scalar_prefetch=1,
        grid=(DEC_BATCH // DEC_PER_STEP,),
        in_specs=[row_spec, row_spec, row_spec,
                  pl.BlockSpec((None, 4, QK_HD), lambda b, pt: (l, 0, 0)),
                  pl.BlockSpec((None, 1, V_HD), lambda b, pt: (l, 0, 0))]
                 + pages * 2,
        out_specs=pl.BlockSpec((DEC_PER_STEP, N_HEADS, V_HD), lambda b, pt: (b, 0, 0)),
    )
    n_ops = DEC_PER_STEP * N_PAGES
    return pl.pallas_call(
        functools.partial(_attn_decode_kernel, lam_init=lam_init),
        grid_spec=grid_spec,
        out_shape=jax.ShapeDtypeStruct((DEC_BATCH, N_HEADS, V_HD), f32),
        compiler_params=pltpu.CompilerParams(dimension_semantics=("arbitrary",),
                                             vmem_limit_bytes=VMEM_LIMIT),
        name=f"attn_decode_{l}",
    )(page_table, q_rep, kn8, vn8, w_lambda, g_subln3,
      *([cache_k] * n_ops), *([cache_v] * n_ops))


def _block_diag(w):
    per = LRU_BLOCKS // 2
    w = w.reshape(DEPTH, 2, per, LRU_BS, LRU_BS)
    eye = jnp.eye(per, dtype=w.dtype)
    return jnp.einsum('dgnij,nm->dgnimj', w, eye).reshape(DEPTH, 2, per * LRU_BS, per * LRU_BS)


@jax.jit
def kernel(x_prompt, x_sample, cache_k, cache_v, page_table, state_conv, state_h, p_prompt, p_sample,
           w_in, w_out, conv_w, conv_b, w_a, b_a, w_i, b_i, a_param, w_lambda, g_subln,
           w_ffn_gate, w_ffn_up, w_ffn_down, w_pe, w_pg, g_norm, g_final):
    n_dec = DEC_BATCH
    pad_rows = N_PAD - N_PROMPT - n_dec
    x = jnp.concatenate([x_prompt.reshape(N_PROMPT, D_MODEL), x_sample.reshape(n_dec, D_MODEL),
                         jnp.zeros((pad_rows, D_MODEL), f32)], axis=0)
    p_all = jnp.concatenate([p_prompt.reshape(DEPTH, N_PROMPT, P_DIM), p_sample.reshape(DEPTH, n_dec, P_DIM),
                             jnp.zeros((DEPTH, pad_rows, P_DIM), f32)], axis=1)

    wg_b = w_ffn_gate.astype(bf16)
    wu_b = w_ffn_up.astype(bf16)
    wd_b = w_ffn_down.astype(bf16)
    win_b = w_in.astype(bf16)
    wo_b = w_out.astype(bf16)
    wpg_b = w_pg.astype(bf16)
    wpe_b = w_pe.astype(bf16)
    wa_b = _block_diag(w_a).astype(bf16)
    wi_b = _block_diag(w_i).astype(bf16)
    g_norm3 = g_norm.reshape(DEPTH, 4, 1, D_MODEL)
    g_final2 = g_final.reshape(1, D_MODEL)
    g_subln3 = g_subln.reshape(DEPTH, 1, V_HD)
    cb3 = conv_b.reshape(DEPTH, 1, LRU_W)
    ba3 = b_a.reshape(DEPTH, 1, LRU_W)
    bi3 = b_i.reshape(DEPTH, 1, LRU_W)
    ap3 = a_param.reshape(DEPTH, 1, LRU_W)
    sc_t = state_conv.transpose(0, 2, 1, 3)
    ck = cache_k.reshape(DEPTH, -1, PAGE_ROWS, V_HD)
    cv = cache_v.reshape(DEPTH, -1, PAGE_ROWS, V_HD)
    gsb = jnp.broadcast_to(g_subln[:, :, None], (DEPTH, V_HD, TQ))
    tabs = _rope_tables()

    ks_s, vs_s, convs_p, hs_p, convs_s, hs_s = [], [], [], [], [], []
    kfin = vfin = None
    dec = slice(N_PROMPT, N_PROMPT + n_dec)
    dec_rows = n_dec * N_HEADS
    for l in range(DEPTH):
        x, u, gate, qs, qt, kb, vt, kfin, vfin, ksn, vsn = _x_call(
            l, x, g_norm3, wg_b, wu_b, wd_b, win_b, tabs, kfin, vfin)
        k_new = ksn[:dec_rows].reshape(n_dec, N_HEADS, V_HD)
        v_new = vsn[:dec_rows].reshape(n_dec, N_HEADS, V_HD)
        ks_s.append(k_new)
        vs_s.append(v_new)

        o_lru, conv_p, h_p = _lru_prompt_call(l, u, gate, jnp.zeros((N_PAD, LRU_W), bf16),
                                              conv_w, cb3, wa_b, wi_b, ba3, bi3, ap3)
        o_lru_s, conv_s, h_s = _lru_decode_call(l, u, gate, sc_t, state_h,
                                                conv_w, cb3, wa_b, wi_b, ba3, bi3, ap3)
        o_lru = lax.dynamic_update_slice(o_lru, o_lru_s, (N_PROMPT, 0))
        convs_p.append(conv_p)
        hs_p.append(h_p.reshape(BATCH, LRU_W))
        convs_s.append(conv_s.transpose(1, 0, 2))
        hs_s.append(h_s)

        o_attn_t = _attn_prompt_call(l, qt, kb, vt, jnp.zeros((ATT_W, N_PAD), bf16), w_lambda, gsb)
        q_rep = jnp.repeat(qs[:n_dec].reshape(n_dec, N_HEADS, V_HD), 2, axis=1)
        pad8 = ((0, 0), (0, 2 * N_HEADS - N_HEADS), (0, 0))
        o_attn_s = _attn_decode_call(l, page_table, q_rep, jnp.pad(k_new, pad8), jnp.pad(v_new, pad8),
                                     ck, cv, w_lambda, g_subln3)
        o_attn_t = lax.dynamic_update_slice(
            o_attn_t, o_attn_s.reshape(n_dec, ATT_W).astype(bf16).T, (0, N_PROMPT))

        x = _y_call(l, x, o_lru, o_attn_t, p_all, wo_b, g_norm3, wg_b, wu_b, wd_b, wpg_b, wpe_b, g_final2)
    y_p, y_s = x
    y_prompt = y_p.reshape(BATCH, SEQ, D_MODEL)
    y_sample = y_s[:n_dec].reshape(n_dec, 1, D_MODEL)
    k_prompt = kfin.reshape(DEPTH, BATCH, SEQ, N_HEADS, V_HD)
    v_prompt = vfin.reshape(DEPTH, BATCH, SEQ, N_HEADS, V_HD)
    k_sample = jnp.stack(ks_s).reshape(DEPTH, n_dec, 1, N_HEADS, V_HD)
    v_sample = jnp.stack(vs_s).reshape(DEPTH, n_dec, 1, N_HEADS, V_HD)
    return (y_prompt, y_sample, k_prompt, v_prompt, jnp.stack(convs_p), jnp.stack(hs_p),
            k_sample, v_sample, jnp.stack(convs_s), jnp.stack(hs_s))
```

```python
import functools
import math

import jax
import jax.numpy as jnp
from jax import lax
from jax.experimental import pallas as pl
from jax.experimental.pallas import tpu as pltpu

f32 = jnp.float32
bf16 = jnp.bfloat16

D_MODEL = 1024
BATCH = 4
SEQ = 4096
DEPTH = 4
DEC_BATCH = 128
PAST_LEN = 2048
PAGE_SIZE = 128
N_PAGES = PAST_LEN // PAGE_SIZE
N_HEADS = 4
PAGE_ROWS = PAGE_SIZE * N_HEADS
LRU_W = 512
LRU_BLOCKS = 8
LRU_BS = 64
CONV_W = 4
C_GATE = 8.0
ATT_W = 512
QK_HD = 64
V_HD = 128
ROT_DIM = 16
ROPE_THETA = 500000.0
D_FF = 2816
P_DIM = 256
EPS = 1e-6
NEG = -1e30
QK_SCALE = QK_HD ** -0.5
LOG2E = math.log2(math.e)

N_PROMPT = BATCH * SEQ
TM = 512
N_TILES_P = N_PROMPT // TM
N_TILES = N_TILES_P + 1
N_PAD = N_TILES * TM
SEQ_TILES = SEQ // TM
FF_CHUNKS = 2
FF_C = D_FF // FF_CHUNKS
TQ = 512
TT = 512
SUB = 8
VMEM_LIMIT = 56 * 1024 * 1024


def _const_spec(shape):
    nd = len(shape)
    return pl.BlockSpec(shape, lambda *_: (0,) * nd, pipeline_mode=pl.Buffered(1))


def _layer_spec(shape, lead):
    nl = len(lead)
    nd = len(shape)
    return pl.BlockSpec((None,) * nl + tuple(shape), lambda *_: tuple(lead) + (0,) * nd,
                        pipeline_mode=pl.Buffered(1))


def _rms(x, g):
    ms = jnp.mean(x * x, axis=-1, keepdims=True)
    return x * lax.rsqrt(ms + EPS) * g


def _swiglu(hb, wg_ref, wu_ref, wd_ref):
    acc = None
    for c in range(FF_CHUNKS):
        sl = slice(c * FF_C, (c + 1) * FF_C)
        g = jnp.dot(hb, wg_ref[:, sl], preferred_element_type=f32)
        u = jnp.dot(hb, wu_ref[:, sl], preferred_element_type=f32)
        a = (g * jax.nn.sigmoid(g) * u).astype(bf16)
        y = jnp.dot(a, wd_ref[sl, :], preferred_element_type=f32)
        acc = y if acc is None else acc + y
    return acc


def _lam(wl_ref, lam_init):
    wl = wl_ref[...]
    s01 = jnp.sum(wl[0:1, :] * wl[1:2, :], axis=-1, keepdims=True)
    s23 = jnp.sum(wl[2:3, :] * wl[3:4, :], axis=-1, keepdims=True)
    return jnp.exp(s01) - jnp.exp(s23) + lam_init


def _rope_table_kernel(freq_ref, c_ref, sa_ref, sb_ref):
    rows = c_ref.shape[0]
    row = lax.broadcasted_iota(jnp.int32, (rows, V_HD), 0)
    lane = lax.broadcasted_iota(jnp.int32, (rows, V_HD), 1)
    pos = jnp.where(row < SEQ, row, PAST_LEN).astype(f32)
    ang = pos * freq_ref[...]
    cos = jnp.cos(ang)
    sin = jnp.sin(ang)
    in_comp = lane % QK_HD
    half = ROT_DIM // 2
    c_ref[...] = jnp.where(in_comp < ROT_DIM, cos, 1.0)
    sa_ref[...] = jnp.where(in_comp < half, -sin, 0.0)
    sb_ref[...] = jnp.where((in_comp >= half) & (in_comp < ROT_DIM), sin, 0.0)


def _rope_tables():
    half = ROT_DIM // 2
    freqs = jnp.power(jnp.float32(ROPE_THETA), -jnp.arange(0, ROT_DIM, 2, dtype=f32) / ROT_DIM)
    lane = jnp.arange(V_HD)
    freq_lane = freqs[(lane % QK_HD) % half].reshape(1, V_HD)
    rows = SEQ + TM
    shp = jax.ShapeDtypeStruct((rows, V_HD), f32)
    return pl.pallas_call(
        _rope_table_kernel,
        out_shape=(shp, shp, shp),
        name="rope_tables",
    )(freq_lane)


def _store_heads_interleaved(dst_ref, t):
    for hd in range(N_HEADS):
        dst_ref[pl.ds(hd, TM, stride=N_HEADS), :] = t[:, hd * V_HD:(hd + 1) * V_HD]


def _token_tile(i, prompt_ref, dec_ref):
    dec = dec_ref[...]
    pad = jnp.zeros((TM - dec.shape[0], dec.shape[1]), dec.dtype)
    return jnp.where(i < N_TILES_P, prompt_ref[...], jnp.concatenate([dec, pad], axis=0))


def _x_kernel(x_ref, xdec_ref, gn0_ref, gn1_ref, wg_ref, wu_ref, wd_ref, win_ref, c_ref, sa_ref, sb_ref,
              *rest, split_input):
    (xo_ref, u_ref, gate_ref, qs_ref, qt_ref, kb_ref, vt_ref,
     kfin_ref, vfin_ref, ksn_ref, vsn_ref) = rest[-11:]
    i = pl.program_id(0)
    x = _token_tile(i, x_ref, xdec_ref) if split_input else x_ref[...]
    h = _rms(x, gn0_ref[...]).astype(bf16)
    x1 = x + 0.5 * _swiglu(h, wg_ref, wu_ref, wd_ref)
    xo_ref[...] = x1
    h1 = _rms(x1, gn1_ref[...]).astype(bf16)

    def proj(i):
        return jnp.dot(h1, win_ref[:, i * LRU_W:(i + 1) * LRU_W], preferred_element_type=f32)

    u_ref[...] = proj(0)
    gate_ref[...] = proj(1)
    cos = c_ref[...]
    sa = sa_ref[...]
    sb = sb_ref[...]

    def rope(t):
        outs = []
        for hd in range(N_HEADS):
            th = t[:, hd * V_HD:(hd + 1) * V_HD]
            up = pltpu.roll(th, V_HD - ROT_DIM // 2, 1)
            dn = pltpu.roll(th, ROT_DIM // 2, 1)
            outs.append(th * cos + up * sa + dn * sb)
        return jnp.concatenate(outs, axis=-1)

    q = rope(proj(2))
    qt_ref[...] = (q * (QK_SCALE * LOG2E)).T.astype(bf16)
    k = rope(proj(3))
    kb_ref[...] = k.astype(bf16)
    v = proj(4)
    vt_ref[...] = v.T.astype(bf16)

    @pl.when(i < N_TILES_P)
    def _():
        _store_heads_interleaved(kfin_ref, k)
        _store_heads_interleaved(vfin_ref, v)

    @pl.when(i == N_TILES_P)
    def _():
        qs_ref[...] = q.astype(bf16)
        _store_heads_interleaved(ksn_ref, k)
        _store_heads_interleaved(vsn_ref, v)


def _x_call(l, x, x_dec, g_norm3, wg, wu, wd, win, tabs, kfin, vfin):
    split_input = x.shape[0] == N_PROMPT
    tile = lambda w: pl.BlockSpec((TM, w), lambda i: (i, 0))
    x_spec = (pl.BlockSpec((TM, D_MODEL), lambda i: (jnp.minimum(i, N_TILES_P - 1), 0))
              if split_input else tile(D_MODEL))
    tile_t = pl.BlockSpec((LRU_W, TM), lambda i: (0, i))
    tab_spec = pl.BlockSpec((TM, V_HD), lambda i: (jnp.where(i < N_TILES_P, i % SEQ_TILES, SEQ_TILES), 0))
    fin_spec = pl.BlockSpec((None, TM * N_HEADS, V_HD), lambda i: (l, jnp.minimum(i, N_TILES_P - 1), 0))
    dec_spec = lambda w: pl.BlockSpec((TM * N_HEADS, w), lambda i: (0, 0))
    act = lambda dt: jax.ShapeDtypeStruct((N_PAD, LRU_W), dt)
    act_t = jax.ShapeDtypeStruct((LRU_W, N_PAD), bf16)
    fin = jax.ShapeDtypeStruct((DEPTH, N_PROMPT * N_HEADS, V_HD), f32)
    dec = jax.ShapeDtypeStruct((TM * N_HEADS, V_HD), f32)
    in_specs = [x_spec, _const_spec(x_dec.shape),
                _layer_spec((1, D_MODEL), (l, 0)), _layer_spec((1, D_MODEL), (l, 1)),
                _layer_spec((D_MODEL, D_FF), (l, 0)), _layer_spec((D_MODEL, D_FF), (l, 0)),
                _layer_spec((D_FF, D_MODEL), (l, 0)), _layer_spec((D_MODEL, 5 * LRU_W), (l,)),
                tab_spec, tab_spec, tab_spec]
    args = [x, x_dec, g_norm3, g_norm3, wg, wu, wd, win, *tabs]
    aliases = {}
    if kfin is not None:
        aliases = {len(args): 7, len(args) + 1: 8}
        in_specs += [pl.BlockSpec(memory_space=pl.ANY)] * 2
        args += [kfin, vfin]
    return pl.pallas_call(
        functools.partial(_x_kernel, split_input=split_input),
        grid=(N_TILES,),
        in_specs=in_specs,
        out_specs=[tile(D_MODEL), tile(LRU_W), tile(LRU_W),
                   pl.BlockSpec((TM, ATT_W), lambda i: (0, 0)), tile_t, tile(ATT_W), tile_t,
                   fin_spec, fin_spec, dec_spec(V_HD), dec_spec(V_HD)],
        out_shape=[jax.ShapeDtypeStruct((N_PAD, D_MODEL), f32), act(f32), act(f32),
                   jax.ShapeDtypeStruct((TM, ATT_W), bf16), act_t, act(bf16), act_t,
                   fin, fin, dec, dec],
        input_output_aliases=aliases,
        compiler_params=pltpu.CompilerParams(dimension_semantics=("arbitrary",),
                                             vmem_limit_bytes=VMEM_LIMIT),
        name=f"ffn1_inproj_{l}",
    )(*args)


def _y_kernel(x_ref, ol_ref, oa_ref, p_ref, pdec_ref, wo_ref, gn2_ref, wg_ref, wu_ref, wd_ref,
              gn3_ref, wpg_ref, wpe_ref, *rest, final):
    x = x_ref[...]
    x2 = (x + jnp.dot(ol_ref[...], wo_ref[:LRU_W, :], preferred_element_type=f32)
          + lax.dot_general(oa_ref[...], wo_ref[LRU_W:, :], (((0,), (0,)), ((), ())),
                            preferred_element_type=f32))
    h = _rms(x2, gn2_ref[...]).astype(bf16)
    x3 = x2 + 0.5 * _swiglu(h, wg_ref, wu_ref, wd_ref)
    h3 = _rms(x3, gn3_ref[...]).astype(bf16)
    gate = jax.nn.sigmoid(jnp.dot(h3, wpg_ref[...], preferred_element_type=f32))
    p = _token_tile(pl.program_id(0), p_ref, pdec_ref)
    pe = jnp.dot(p.astype(bf16), wpe_ref[...], preferred_element_type=f32)
    x4 = x3 + gate * pe
    if final:
        gf_ref, yp_ref, ys_ref = rest
        y = _rms(x4, gf_ref[...])
        i = pl.program_id(0)

        @pl.when(i < N_TILES_P)
        def _():
            yp_ref[...] = y

        @pl.when(i == N_TILES_P)
        def _():
            ys_ref[...] = y
    else:
        (xo_ref,) = rest
        xo_ref[...] = x4


def _y_call(l, x, o_lru, o_attn, p_prompt, p_dec, wo, g_norm3, wg, wu, wd, wpg, wpe, g_final):
    final = l == DEPTH - 1
    tile = lambda w: pl.BlockSpec((TM, w), lambda i: (i, 0))
    in_specs = [tile(D_MODEL), tile(LRU_W), pl.BlockSpec((ATT_W, TM), lambda i: (0, i)),
                pl.BlockSpec((None, TM, P_DIM), lambda i: (l, jnp.minimum(i, N_TILES_P - 1), 0)),
                pl.BlockSpec((None, DEC_BATCH, P_DIM), lambda i: (l, 0, 0)),
                _layer_spec((D_MODEL, D_MODEL), (l,)),
                _layer_spec((1, D_MODEL), (l, 2)),
                _layer_spec((D_MODEL, D_FF), (l, 1)), _layer_spec((D_MODEL, D_FF), (l, 1)),
                _layer_spec((D_FF, D_MODEL), (l, 1)),
                _layer_spec((1, D_MODEL), (l, 3)),
                _layer_spec((D_MODEL, D_MODEL), (l,)), _layer_spec((P_DIM, D_MODEL), (l,))]
    args = [x, o_lru, o_attn, p_prompt, p_dec, wo, g_norm3, wg, wu, wd, g_norm3, wpg, wpe]
    out_specs = tile(D_MODEL)
    out_shape = jax.ShapeDtypeStruct((N_PAD, D_MODEL), f32)
    if final:
        in_specs.append(_const_spec((1, D_MODEL)))
        args.append(g_final)
        out_specs = [pl.BlockSpec((TM, D_MODEL), lambda i: (jnp.minimum(i, N_TILES_P - 1), 0)),
                     pl.BlockSpec((TM, D_MODEL), lambda i: (0, 0))]
        out_shape = [jax.ShapeDtypeStruct((N_PROMPT, D_MODEL), f32),
                     jax.ShapeDtypeStruct((TM, D_MODEL), f32)]
    return pl.pallas_call(
        functools.partial(_y_kernel, final=final),
        grid=(N_TILES,),
        in_specs=in_specs,
        out_specs=out_specs,
        out_shape=out_shape,
        compiler_params=pltpu.CompilerParams(dimension_semantics=("arbitrary",),
                                             vmem_limit_bytes=VMEM_LIMIT),
        name=f"outproj_ffn2_embed_{l}",
    )(*args)


def _softplus(x):
    return jnp.maximum(x, 0.0) + jnp.log1p(jnp.exp(-jnp.abs(x)))


def _lru_gates(xc, wa_ref, wi_ref, ba_ref, bi_ref, ap_ref):
    xb = xc.astype(bf16)
    half = LRU_W // 2

    def bd(w_ref):
        lo = jnp.dot(xb[:, :half], w_ref[0], preferred_element_type=f32)
        hi = jnp.dot(xb[:, half:], w_ref[1], preferred_element_type=f32)
        return jnp.concatenate([lo, hi], axis=-1)

    r = jax.nn.sigmoid(bd(wa_ref) + ba_ref[...])
    i = jax.nn.sigmoid(bd(wi_ref) + bi_ref[...])
    log_a = -C_GATE * r * _softplus(-ap_ref[...])
    a = jnp.exp(log_a)
    b = jnp.sqrt(-jnp.tanh(log_a) * (a * a + 1.0)) * (i * xc)
    return a, b


def _lru_prompt_kernel(u_ref, gate_ref, cw_ref, cb_ref, wa_ref, wi_ref, ba_ref, bi_ref, ap_ref,
                       oinit_ref, o_ref, conv_ref, hlast_ref, ubuf, abuf, bbuf, hcar):
    del oinit_ref
    t = pl.program_id(1)

    @pl.when(t == 0)
    def _():
        ubuf[0:SUB, :] = jnp.zeros((SUB, LRU_W), f32)
        hcar[...] = jnp.zeros((1, LRU_W), f32)

    ubuf[SUB:SUB + TT, :] = u_ref[...]
    xc = cb_ref[...]
    for j in range(CONV_W):
        off = SUB - (CONV_W - 1) + j
        xc = xc + ubuf[off:off + TT, :] * cw_ref[j:j + 1, :]
    conv_ref[...] = ubuf[TT + SUB - (CONV_W - 1):TT + SUB, :]
    ubuf[0:SUB, :] = ubuf[TT:TT + SUB, :]

    a, b = _lru_gates(xc, wa_ref, wi_ref, ba_ref, bi_ref, ap_ref)
    abuf[...] = a
    bbuf[...] = b

    row = lax.broadcasted_iota(jnp.int32, (SUB, LRU_W), 0)

    def group(g, h_prev):
        off = pl.multiple_of(g * SUB, SUB)
        ag = abuf[pl.ds(off, SUB), :]
        bg = bbuf[pl.ds(off, SUB), :]
        for d in (1, 2, 4):
            keep = row >= d
            a_sh = pltpu.roll(ag, d, 0)
            b_sh = pltpu.roll(bg, d, 0)
            bg = jnp.where(keep, ag * b_sh + bg, bg)
            ag = jnp.where(keep, ag * a_sh, ag)
        hg = ag * h_prev + bg
        bbuf[pl.ds(off, SUB), :] = hg
        return hg[SUB - 1:SUB, :]

    h_last = lax.fori_loop(0, TT // SUB, group, hcar[...])
    hcar[...] = h_last
    hlast_ref[...] = h_last
    o_ref[...] = (bbuf[...] * jax.nn.gelu(gate_ref[...])).astype(bf16)


def _lru_prompt_call(l, u, gate, o_init, cw, cb, wa, wi, ba, bi, ap):
    tile = pl.BlockSpec((TT, LRU_W), lambda b, t: (b * (SEQ // TT) + t, 0))
    vec = lambda: pl.BlockSpec((None, 1, LRU_W), lambda b, t: (l, 0, 0))
    return pl.pallas_call(
        _lru_prompt_kernel,
        grid=(BATCH, SEQ // TT),
        in_specs=[tile, tile,
                  pl.BlockSpec((None, CONV_W, LRU_W), lambda b, t: (l, 0, 0)), vec(),
                  pl.BlockSpec((None, 2, LRU_W // 2, LRU_W // 2), lambda b, t: (l, 0, 0, 0)),
                  pl.BlockSpec((None, 2, LRU_W // 2, LRU_W // 2), lambda b, t: (l, 0, 0, 0)),
                  vec(), vec(), vec(),
                  pl.BlockSpec(memory_space=pl.ANY)],
        out_specs=[tile,
                   pl.BlockSpec((None, CONV_W - 1, LRU_W), lambda b, t: (b, 0, 0)),
                   pl.BlockSpec((None, 1, LRU_W), lambda b, t: (b, 0, 0))],
        out_shape=[jax.ShapeDtypeStruct((N_PAD, LRU_W), bf16),
                   jax.ShapeDtypeStruct((BATCH, CONV_W - 1, LRU_W), f32),
                   jax.ShapeDtypeStruct((BATCH, 1, LRU_W), f32)],
        scratch_shapes=[pltpu.VMEM((TT + SUB, LRU_W), f32), pltpu.VMEM((TT, LRU_W), f32),
                        pltpu.VMEM((TT, LRU_W), f32), pltpu.VMEM((1, LRU_W), f32)],
        input_output_aliases={9: 0},
        compiler_params=pltpu.CompilerParams(dimension_semantics=("arbitrary", "arbitrary"),
                                             vmem_limit_bytes=VMEM_LIMIT),
        name=f"rglru_prompt_{l}",
    )(u, gate, cw, cb, wa, wi, ba, bi, ap, o_init)


def _lru_decode_kernel(u_ref, gate_ref, sc_ref, h0_ref, cw_ref, cb_ref, wa_ref, wi_ref,
                       ba_ref, bi_ref, ap_ref, o_ref, conv_ref, h_ref):
    u = u_ref[...]
    xc = cb_ref[...]
    for j in range(CONV_W - 1):
        xc = xc + sc_ref[j] * cw_ref[j:j + 1, :]
    xc = xc + u * cw_ref[CONV_W - 1:CONV_W, :]
    a, b = _lru_gates(xc, wa_ref, wi_ref, ba_ref, bi_ref, ap_ref)
    h = a * h0_ref[...] + b
    h_ref[...] = h
    for j in range(CONV_W - 2):
        conv_ref[j] = sc_ref[j + 1]
    conv_ref[CONV_W - 2] = u
    o_ref[...] = (h * jax.nn.gelu(gate_ref[...])).astype(bf16)


def _lru_decode_call(l, u, gate, sc_t, h0, cw, cb, wa, wi, ba, bi, ap):
    rows = pl.BlockSpec((DEC_BATCH, LRU_W), lambda i: (N_PROMPT // DEC_BATCH, 0))
    vec = lambda: pl.BlockSpec((None, 1, LRU_W), lambda i: (l, 0, 0))
    return pl.pallas_call(
        _lru_decode_kernel,
        grid=(1,),
        in_specs=[rows, rows,
                  pl.BlockSpec((None, CONV_W - 1, DEC_BATCH, LRU_W), lambda i: (l, 0, 0, 0)),
                  pl.BlockSpec((None, DEC_BATCH, LRU_W), lambda i: (l, 0, 0)),
                  pl.BlockSpec((None, CONV_W, LRU_W), lambda i: (l, 0, 0)), vec(),
                  pl.BlockSpec((None, 2, LRU_W // 2, LRU_W // 2), lambda i: (l, 0, 0, 0)),
                  pl.BlockSpec((None, 2, LRU_W // 2, LRU_W // 2), lambda i: (l, 0, 0, 0)),
                  vec(), vec(), vec()],
        out_specs=[pl.BlockSpec((DEC_BATCH, LRU_W), lambda i: (0, 0)),
                   pl.BlockSpec((CONV_W - 1, DEC_BATCH, LRU_W), lambda i: (0, 0, 0)),
                   pl.BlockSpec((DEC_BATCH, LRU_W), lambda i: (0, 0))],
        out_shape=[jax.ShapeDtypeStruct((DEC_BATCH, LRU_W), bf16),
                   jax.ShapeDtypeStruct((CONV_W - 1, DEC_BATCH, LRU_W), f32),
                   jax.ShapeDtypeStruct((DEC_BATCH, LRU_W), f32)],
        name=f"rglru_decode_{l}",
    )(u, gate, sc_t, h0, cw, cb, wa, wi, ba, bi, ap)


def _subln(o, gs, lam_init):
    ms = jnp.mean(o * o, axis=-1, keepdims=True)
    return (o * lax.rsqrt(ms + EPS) * gs) * (1.0 - lam_init)


def _attn_prompt_kernel(qt_ref, k_ref, vt_ref, wl_ref, gsb_ref, oinit_ref, o_ref,
                        s_buf0, s_buf1, m1, a1, m2, a2, *, lam_init):
    del oinit_ref
    qi = pl.program_id(2)
    q = qt_ref[...].astype(f32)
    comp_row = lax.broadcasted_iota(jnp.int32, (V_HD, TQ), 0)
    qs = (jnp.where(comp_row < QK_HD, q, 0.0).astype(bf16),
          jnp.where(comp_row >= QK_HD, q, 0.0).astype(bf16))
    stats = ((m1, a1), (m2, a2))
    for m_ref, a_ref in stats:
        m_ref[...] = jnp.full((1, TQ), NEG, f32)
        a_ref[...] = jnp.zeros((V_HD + SUB, TQ), f32)
    ones = jnp.ones((SUB, TQ), bf16)

    def scores(off, s_ref, masked):
        kb = k_ref[pl.ds(pl.multiple_of(off, TQ), TQ), :]
        for c in range(2):
            s = jnp.dot(kb, qs[c], preferred_element_type=f32)
            if masked:
                key = lax.broadcasted_iota(jnp.int32, (TQ, TQ), 0)
                qry = lax.broadcasted_iota(jnp.int32, (TQ, TQ), 1)
                s = jnp.where(key <= qry, s, NEG)
            s_ref[c] = s

    def consume(off, s_ref):
        vt = jnp.concatenate([vt_ref[:, pl.ds(pl.multiple_of(off, TQ), TQ)], ones], axis=0)
        for c, (m_ref, a_ref) in enumerate(stats):
            s = s_ref[c]
            m_old = m_ref[...]
            m_new = jnp.maximum(m_old, jnp.max(s, axis=0, keepdims=True))
            alpha = jnp.exp2(m_old - m_new)
            p = jnp.exp2(s - m_new).astype(bf16)
            a_ref[...] = alpha * a_ref[...] + jnp.dot(vt, p, preferred_element_type=f32)
            m_ref[...] = m_new

    diag = qi * TQ
    scores(diag, s_buf0, True)

    def pair(u, prev_off):
        off1 = 2 * u * TQ
        scores(off1, s_buf1, False)
        consume(prev_off, s_buf0)
        off2 = off1 + TQ
        scores(off2, s_buf0, False)
        consume(off1, s_buf1)
        return off2

    last_off = lax.fori_loop(0, qi // 2, pair, diag)

    @pl.when(qi % 2 == 1)
    def _():
        off = (qi - 1) * TQ
        scores(off, s_buf1, False)
        consume(last_off, s_buf0)
        consume(off, s_buf1)

    @pl.when(qi % 2 == 0)
    def _():
        consume(last_off, s_buf0)

    lam = _lam(wl_ref, lam_init)
    inv1 = 1.0 / a1[V_HD:V_HD + 1, :]
    inv2 = 1.0 / a2[V_HD:V_HD + 1, :]
    o = a1[:V_HD, :] * inv1 - lam * (a2[:V_HD, :] * inv2)
    ms = jnp.mean(o * o, axis=0, keepdims=True)
    y = (o * lax.rsqrt(ms + EPS) * gsb_ref[...]) * (1.0 - lam_init)
    o_ref[...] = y.astype(bf16)


def _attn_prompt_call(l, qt, kb, vt, o_init, w_lambda, gsb):
    lam_init = 0.8 - 0.6 * math.exp(-0.3 * l)
    nq = SEQ // TQ
    qo_spec = pl.BlockSpec((V_HD, TQ), lambda b, h, i: (h, b * nq + i))
    return pl.pallas_call(
        functools.partial(_attn_prompt_kernel, lam_init=lam_init),
        grid=(BATCH, N_HEADS, nq),
        in_specs=[qo_spec,
                  pl.BlockSpec((SEQ, V_HD), lambda b, h, i: (b, h)),
                  pl.BlockSpec((V_HD, SEQ), lambda b, h, i: (h, b)),
                  pl.BlockSpec((None, 4, QK_HD), lambda b, h, i: (l, 0, 0)),
                  pl.BlockSpec((None, V_HD, TQ), lambda b, h, i: (l, 0, 0)),
                  pl.BlockSpec(memory_space=pl.ANY)],
        out_specs=qo_spec,
        out_shape=jax.ShapeDtypeStruct((ATT_W, N_PAD), bf16),
        scratch_shapes=[pltpu.VMEM((2, TQ, TQ), f32), pltpu.VMEM((2, TQ, TQ), f32),
                        pltpu.VMEM((1, TQ), f32), pltpu.VMEM((V_HD + SUB, TQ), f32),
                        pltpu.VMEM((1, TQ), f32), pltpu.VMEM((V_HD + SUB, TQ), f32)],
        input_output_aliases={5: 0},
        compiler_params=pltpu.CompilerParams(
            dimension_semantics=("arbitrary", "arbitrary", "arbitrary"),
            vmem_limit_bytes=VMEM_LIMIT),
        name=f"attn_prompt_{l}",
    )(qt, kb, vt, w_lambda, gsb, o_init)


def _attn_decode_kernel(pt_ref, q_ref, kn_ref, vn_ref, wl_ref, gs_ref, *rest, lam_init):
    del pt_ref
    k_refs = rest[:N_PAGES]
    v_refs = rest[N_PAGES:2 * N_PAGES]
    o_ref, e_s, enew_s, coef_s = rest[2 * N_PAGES:]

    @pl.when(pl.program_id(0) == 0)
    def _():
        e_s[...] = jnp.zeros_like(e_s)
        enew_s[...] = jnp.zeros_like(enew_s)
        coef_s[...] = jnp.zeros_like(coef_s)

    o = _decode_values(e_s[...], enew_s[...], coef_s[...], vn_ref[...], v_refs)
    o_ref[...] = _subln(o, gs_ref[...], lam_init)
    e, e_new, coef = _decode_scores(q_ref[...].astype(f32), kn_ref[...], k_refs, _lam(wl_ref, lam_init))
    e_s[...] = e
    enew_s[...] = e_new
    coef_s[...] = coef


_NT = (((1,), (1,)), ((), ()))
_NC = 2 * N_HEADS


def _decode_scores(q, kn, k_refs, lam):
    nc = _NC
    nt = _NT
    row8 = lax.broadcasted_iota(jnp.int32, (nc, V_HD), 0)
    lane8 = lax.broadcasted_iota(jnp.int32, (nc, V_HD), 1)
    lane1 = lax.broadcasted_iota(jnp.int32, (1, V_HD), 1)
    q8 = jnp.where(jnp.right_shift(lane8, 6) == jnp.bitwise_and(row8, 1), q, 0.0)
    q8 = q8 * QK_SCALE
    q_rows = jnp.concatenate([q8] * (V_HD // nc), axis=0)
    q_rows = jnp.concatenate([q_rows] * N_PAGES, axis=1)
    rq = lax.broadcasted_iota(jnp.int32, (V_HD, N_PAGES * V_HD), 0)
    lq = lax.broadcasted_iota(jnp.int32, (V_HD, N_PAGES * V_HD), 1)
    qmat = jnp.where(jnp.right_shift(rq, 3) == jnp.right_shift(lq, 7), q_rows, 0.0).astype(bf16)

    half = N_PAGES // 2
    s = None
    for g in range(2):
        kcat = jnp.concatenate([k_refs[p][...].astype(bf16) for p in range(g * half, (g + 1) * half)],
                               axis=1)
        sg = lax.dot_general(kcat, qmat[:, g * half * V_HD:(g + 1) * half * V_HD], nt,
                             preferred_element_type=f32)
        s = sg if s is None else s + sg
    rowp = lax.broadcasted_iota(jnp.int32, (PAGE_ROWS, V_HD), 0)
    colp = lax.broadcasted_iota(jnp.int32, (PAGE_ROWS, V_HD), 1)
    own = jnp.bitwise_and(rowp, N_HEADS - 1) == jnp.right_shift(jnp.bitwise_and(colp, nc - 1), 1)
    s = jnp.where(own, s, NEG)
    s_new = lax.dot_general(kn.astype(bf16), qmat[:, :V_HD], nt, preferred_element_type=f32)
    own8 = (jnp.where(row8 < N_HEADS, row8, -1)
            == jnp.where(lane8 < nc, jnp.right_shift(lane8, 1), -2))
    s_new = jnp.where(own8, s_new, NEG)

    def over_pages(x, op):
        x = jnp.broadcast_to(x, (SUB, V_HD))
        for sh in (nc, 2 * nc, 4 * nc, 8 * nc):
            x = op(x, pltpu.roll(x, sh, 1))
        return x[0:1, :]

    m = over_pages(jnp.maximum(jnp.max(s, axis=0, keepdims=True),
                               jnp.max(s_new, axis=0, keepdims=True)), jnp.maximum)
    e = jnp.exp(s - m)
    e_new = jnp.exp(s_new - m)
    denom = over_pages(jnp.sum(e, axis=0, keepdims=True) + jnp.sum(e_new, axis=0, keepdims=True),
                       jnp.add)
    coef = jnp.where(jnp.bitwise_and(lane1, 1) == 0, 1.0, -lam) / denom
    return e.astype(bf16), e_new.astype(bf16), coef


def _decode_values(e, e_new, coef, vn, v_refs):
    nc = _NC
    lane1 = lax.broadcasted_iota(jnp.int32, (1, V_HD), 1)

    def lane_weights(first_lane):
        sel = jnp.where(jnp.right_shift(lane1, 3) == first_lane // nc, coef, 0.0)
        return jnp.broadcast_to(sel, (V_HD, V_HD)).astype(bf16)

    w_new = lax.dot_general(e_new, lane_weights(0), _NT, preferred_element_type=f32)
    acc = w_new * vn
    for p in range(0, N_PAGES, 2):
        both = jnp.concatenate([lane_weights(nc * p), lane_weights(nc * (p + 1))], axis=0)
        w = lax.dot_general(e, both, _NT, preferred_element_type=f32)
        wv = w[:, :V_HD] * v_refs[p][...] + w[:, V_HD:] * v_refs[p + 1][...]
        acc = acc + jnp.sum(wv.reshape(PAGE_ROWS // SUB, SUB, V_HD), axis=0)
    return acc[:N_HEADS, :] + acc[N_HEADS:, :]


def _attn_decode_call(l, page_table, q_rep, kn8, vn8, cache_k, cache_v, w_lambda, g_subln3):
    lam_init = 0.8 - 0.6 * math.exp(-0.3 * l)
    last = DEC_BATCH - 1
    cur = lambda t: jnp.minimum(t, last)
    prev = lambda t: jnp.maximum(t - 1, 0)
    row_spec = lambda tok: pl.BlockSpec((None, _NC, V_HD), lambda t, pt: (tok(t), 0, 0))

    def page_spec(tok, p):
        return pl.BlockSpec((None, None, PAGE_ROWS, V_HD), lambda t, pt: (l, pt[tok(t), p], 0, 0))

    grid_spec = pltpu.PrefetchScalarGridSpec(
        num_scalar_prefetch=1,
        grid=(DEC_BATCH + 1,),
        in_specs=[row_spec(cur), row_spec(cur), row_spec(prev),
                  pl.BlockSpec((None, 4, QK_HD), lambda t, pt: (l, 0, 0)),
                  pl.BlockSpec((None, 1, V_HD), lambda t, pt: (l, 0, 0))]
                 + [page_spec(cur, p) for p in range(N_PAGES)]
                 + [page_spec(prev, p) for p in range(N_PAGES)],
        out_specs=pl.BlockSpec((None, N_HEADS, V_HD), lambda t, pt: (prev(t), 0, 0)),
        scratch_shapes=[pltpu.VMEM((PAGE_ROWS, V_HD), bf16), pltpu.VMEM((_NC, V_HD), bf16),
                        pltpu.VMEM((1, V_HD), f32)],
    )
    return pl.pallas_call(
        functools.partial(_attn_decode_kernel, lam_init=lam_init),
        grid_spec=grid_spec,
        out_shape=jax.ShapeDtypeStruct((DEC_BATCH, N_HEADS, V_HD), f32),
        compiler_params=pltpu.CompilerParams(dimension_semantics=("arbitrary",),
                                             vmem_limit_bytes=VMEM_LIMIT),
        name=f"attn_decode_{l}",
    )(page_table, q_rep, kn8, vn8, w_lambda, g_subln3,
      *([cache_k] * N_PAGES), *([cache_v] * N_PAGES))


def _block_diag(w):
    per = LRU_BLOCKS // 2
    w = w.reshape(DEPTH, 2, per, LRU_BS, LRU_BS)
    eye = jnp.eye(per, dtype=w.dtype)
    return jnp.einsum('dgnij,nm->dgnimj', w, eye).reshape(DEPTH, 2, per * LRU_BS, per * LRU_BS)


@jax.jit
def kernel(x_prompt, x_sample, cache_k, cache_v, page_table, state_conv, state_h, p_prompt, p_sample,
           w_in, w_out, conv_w, conv_b, w_a, b_a, w_i, b_i, a_param, w_lambda, g_subln,
           w_ffn_gate, w_ffn_up, w_ffn_down, w_pe, w_pg, g_norm, g_final):
    n_dec = DEC_BATCH
    x = x_prompt.reshape(N_PROMPT, D_MODEL)
    x_dec = x_sample.reshape(n_dec, D_MODEL)
    p_pr = p_prompt.reshape(DEPTH, N_PROMPT, P_DIM)
    p_dec = p_sample.reshape(DEPTH, n_dec, P_DIM)

    wg_b = w_ffn_gate.astype(bf16)
    wu_b = w_ffn_up.astype(bf16)
    wd_b = w_ffn_down.astype(bf16)
    win_b = w_in.astype(bf16)
    wo_b = w_out.astype(bf16)
    wpg_b = w_pg.astype(bf16)
    wpe_b = w_pe.astype(bf16)
    wa_b = _block_diag(w_a).astype(bf16)
    wi_b = _block_diag(w_i).astype(bf16)
    g_norm3 = g_norm.reshape(DEPTH, 4, 1, D_MODEL)
    g_final2 = g_final.reshape(1, D_MODEL)
    g_subln3 = g_subln.reshape(DEPTH, 1, V_HD)
    cb3 = conv_b.reshape(DEPTH, 1, LRU_W)
    ba3 = b_a.reshape(DEPTH, 1, LRU_W)
    bi3 = b_i.reshape(DEPTH, 1, LRU_W)
    ap3 = a_param.reshape(DEPTH, 1, LRU_W)
    sc_t = state_conv.transpose(0, 2, 1, 3)
    ck = cache_k.reshape(DEPTH, -1, PAGE_ROWS, V_HD)
    cv = cache_v.reshape(DEPTH, -1, PAGE_ROWS, V_HD)
    gsb = jnp.broadcast_to(g_subln[:, :, None], (DEPTH, V_HD, TQ))
    tabs = _rope_tables()

    ks_s, vs_s, convs_p, hs_p, convs_s, hs_s = [], [], [], [], [], []
    kfin = vfin = None
    dec_rows = n_dec * N_HEADS
    for l in range(DEPTH):
        x, u, gate, qs, qt, kb, vt, kfin, vfin, ksn, vsn = _x_call(
            l, x, x_dec, g_norm3, wg_b, wu_b, wd_b, win_b, tabs, kfin, vfin)
        k_new = ksn[:dec_rows].reshape(n_dec, N_HEADS, V_HD)
        v_new = vsn[:dec_rows].reshape(n_dec, N_HEADS, V_HD)
        ks_s.append(k_new)
        vs_s.append(v_new)

        o_lru, conv_p, h_p = _lru_prompt_call(l, u, gate, jnp.zeros((N_PAD, LRU_W), bf16),
                                              conv_w, cb3, wa_b, wi_b, ba3, bi3, ap3)
        o_lru_s, conv_s, h_s = _lru_decode_call(l, u, gate, sc_t, state_h,
                                                conv_w, cb3, wa_b, wi_b, ba3, bi3, ap3)
        o_lru = lax.dynamic_update_slice(o_lru, o_lru_s, (N_PROMPT, 0))
        convs_p.append(conv_p)
        hs_p.append(h_p.reshape(BATCH, LRU_W))
        convs_s.append(conv_s.transpose(1, 0, 2))
        hs_s.append(h_s)

        o_attn_t = _attn_prompt_call(l, qt, kb, vt, jnp.zeros((ATT_W, N_PAD), bf16), w_lambda, gsb)
        q_rep = jnp.repeat(qs[:n_dec].reshape(n_dec, N_HEADS, V_HD), 2, axis=1)
        pad8 = ((0, 0), (0, 2 * N_HEADS - N_HEADS), (0, 0))
        o_attn_s = _attn_decode_call(l, page_table, q_rep, jnp.pad(k_new, pad8), jnp.pad(v_new, pad8),
                                     ck, cv, w_lambda, g_subln3)
        o_attn_t = lax.dynamic_update_slice(
            o_attn_t, o_attn_s.reshape(n_dec, ATT_W).astype(bf16).T, (0, N_PROMPT))

        x = _y_call(l, x, o_lru, o_attn_t, p_pr, p_dec, wo_b, g_norm3, wg_b, wu_b, wd_b, wpg_b, wpe_b,
                    g_final2)
    y_p, y_s = x
    y_prompt = y_p.reshape(BATCH, SEQ, D_MODEL)
    y_sample = y_s[:n_dec].reshape(n_dec, 1, D_MODEL)
    k_prompt = kfin.reshape(DEPTH, BATCH, SEQ, N_HEADS, V_HD)
    v_prompt = vfin.reshape(DEPTH, BATCH, SEQ, N_HEADS, V_HD)
    k_sample = jnp.stack(ks_s).reshape(DEPTH, n_dec, 1, N_HEADS, V_HD)
    v_sample = jnp.stack(vs_s).reshape(DEPTH, n_dec, 1, N_HEADS, V_HD)
    return (y_prompt, y_sample, k_prompt, v_prompt, jnp.stack(convs_p), jnp.stack(hs_p),
            k_sample, v_sample, jnp.stack(convs_s), jnp.stack(hs_s))
```

```python
import functools
import math

import jax
import jax.numpy as jnp
from jax import lax
from jax.experimental import pallas as pl
from jax.experimental.pallas import tpu as pltpu

f32 = jnp.float32
bf16 = jnp.bfloat16

D_MODEL = 1024
BATCH = 4
SEQ = 4096
DEPTH = 4
DEC_BATCH = 128
PAST_LEN = 2048
PAGE_SIZE = 128
N_PAGES = PAST_LEN // PAGE_SIZE
N_HEADS = 4
PAGE_ROWS = PAGE_SIZE * N_HEADS
LRU_W = 512
LRU_BLOCKS = 8
LRU_BS = 64
CONV_W = 4
C_GATE = 8.0
ATT_W = 512
QK_HD = 64
V_HD = 128
ROT_DIM = 16
ROPE_THETA = 500000.0
D_FF = 2816
P_DIM = 256
EPS = 1e-6
NEG = -1e30
QK_SCALE = QK_HD ** -0.5
LOG2E = math.log2(math.e)

N_PROMPT = BATCH * SEQ
TM = 512
N_TILES_P = N_PROMPT // TM
N_TILES = N_TILES_P + 1
N_PAD = N_TILES * TM
SEQ_TILES = SEQ // TM
MXU_COLS = 256
FF_SPLIT = (D_FF // MXU_COLS + 1) // 2 * MXU_COLS
FF_CHUNKS = ((0, FF_SPLIT), (FF_SPLIT, D_FF))
TQ = 512
TT = 512
SUB = 8
VMEM_LIMIT = 56 * 1024 * 1024


def _const_spec(shape):
    nd = len(shape)
    return pl.BlockSpec(shape, lambda *_: (0,) * nd, pipeline_mode=pl.Buffered(1))


def _layer_spec(shape, lead):
    nl = len(lead)
    nd = len(shape)
    return pl.BlockSpec((None,) * nl + tuple(shape), lambda *_: tuple(lead) + (0,) * nd,
                        pipeline_mode=pl.Buffered(1))


def _rms(x, g):
    ms = jnp.mean(x * x, axis=-1, keepdims=True)
    return x * lax.rsqrt(ms + EPS) * g


def _swiglu(hb, wg_ref, wu_ref, wd_ref):
    acc = None
    for lo, hi in FF_CHUNKS:
        sl = slice(lo, hi)
        g = jnp.dot(hb, wg_ref[:, sl], preferred_element_type=f32)
        u = jnp.dot(hb, wu_ref[:, sl], preferred_element_type=f32)
        a = (g * jax.nn.sigmoid(g) * u).astype(bf16)
        y = jnp.dot(a, wd_ref[sl, :], preferred_element_type=f32)
        acc = y if acc is None else acc + y
    return acc


def _lam(wl_ref, lam_init):
    wl = wl_ref[...]
    s01 = jnp.sum(wl[0:1, :] * wl[1:2, :], axis=-1, keepdims=True)
    s23 = jnp.sum(wl[2:3, :] * wl[3:4, :], axis=-1, keepdims=True)
    return jnp.exp(s01) - jnp.exp(s23) + lam_init


def _rope_table_kernel(freq_ref, c_ref, sa_ref, sb_ref):
    rows = c_ref.shape[0]
    row = lax.broadcasted_iota(jnp.int32, (rows, V_HD), 0)
    lane = lax.broadcasted_iota(jnp.int32, (rows, V_HD), 1)
    pos = jnp.where(row < SEQ, row, PAST_LEN).astype(f32)
    ang = pos * freq_ref[...]
    cos = jnp.cos(ang)
    sin = jnp.sin(ang)
    in_comp = lane % QK_HD
    half = ROT_DIM // 2
    c_ref[...] = jnp.where(in_comp < ROT_DIM, cos, 1.0)
    sa_ref[...] = jnp.where(in_comp < half, -sin, 0.0)
    sb_ref[...] = jnp.where((in_comp >= half) & (in_comp < ROT_DIM), sin, 0.0)


def _rope_tables():
    half = ROT_DIM // 2
    freqs = jnp.power(jnp.float32(ROPE_THETA), -jnp.arange(0, ROT_DIM, 2, dtype=f32) / ROT_DIM)
    lane = jnp.arange(V_HD)
    freq_lane = freqs[(lane % QK_HD) % half].reshape(1, V_HD)
    rows = SEQ + TM
    shp = jax.ShapeDtypeStruct((rows, V_HD), f32)
    return pl.pallas_call(
        _rope_table_kernel,
        out_shape=(shp, shp, shp),
        name="rope_tables",
    )(freq_lane)


def _store_heads_interleaved(dst_ref, t):
    for hd in range(N_HEADS):
        dst_ref[pl.ds(hd, TM, stride=N_HEADS), :] = t[:, hd * V_HD:(hd + 1) * V_HD]


def _token_tile(i, prompt_ref, dec_ref):
    dec = dec_ref[...]
    pad = jnp.zeros((TM - dec.shape[0], dec.shape[1]), dec.dtype)
    return jnp.where(i < N_TILES_P, prompt_ref[...], jnp.concatenate([dec, pad], axis=0))


def _x_kernel(x_ref, xdec_ref, gn0_ref, gn1_ref, wg_ref, wu_ref, wd_ref, win_ref, c_ref, sa_ref, sb_ref,
              *rest, split_input):
    (xo_ref, u_ref, gate_ref, qs_ref, qt_ref, kb_ref, vt_ref,
     kfin_ref, vfin_ref, ksn_ref, vsn_ref) = rest[-11:]
    i = pl.program_id(0)
    x = _token_tile(i, x_ref, xdec_ref) if split_input else x_ref[...]
    h = _rms(x, gn0_ref[...]).astype(bf16)
    x1 = x + 0.5 * _swiglu(h, wg_ref, wu_ref, wd_ref)
    xo_ref[...] = x1
    h1 = _rms(x1, gn1_ref[...]).astype(bf16)

    def proj(i):
        return jnp.dot(h1, win_ref[:, i * LRU_W:(i + 1) * LRU_W], preferred_element_type=f32)

    u_ref[...] = proj(0)
    gate_ref[...] = proj(1)
    cos = c_ref[...]
    sa = sa_ref[...]
    sb = sb_ref[...]

    def rope(t):
        outs = []
        for hd in range(N_HEADS):
            th = t[:, hd * V_HD:(hd + 1) * V_HD]
            up = pltpu.roll(th, V_HD - ROT_DIM // 2, 1)
            dn = pltpu.roll(th, ROT_DIM // 2, 1)
            outs.append(th * cos + up * sa + dn * sb)
        return jnp.concatenate(outs, axis=-1)

    q = rope(proj(2))
    qt_ref[...] = (q * (QK_SCALE * LOG2E)).T.astype(bf16)
    k = rope(proj(3))
    kb_ref[...] = k.astype(bf16)
    v = proj(4)
    vt_ref[...] = v.T.astype(bf16)

    @pl.when(i < N_TILES_P)
    def _():
        _store_heads_interleaved(kfin_ref, k)
        _store_heads_interleaved(vfin_ref, v)

    @pl.when(i == N_TILES_P)
    def _():
        qs_ref[...] = q.astype(bf16)
        _store_heads_interleaved(ksn_ref, k)
        _store_heads_interleaved(vsn_ref, v)


def _x_call(l, x, x_dec, g_norm3, wg, wu, wd, win, tabs, kfin, vfin):
    split_input = x.shape[0] == N_PROMPT
    tile = lambda w: pl.BlockSpec((TM, w), lambda i: (i, 0))
    x_spec = (pl.BlockSpec((TM, D_MODEL), lambda i: (jnp.minimum(i, N_TILES_P - 1), 0))
              if split_input else tile(D_MODEL))
    tile_t = pl.BlockSpec((LRU_W, TM), lambda i: (0, i))
    tab_spec = pl.BlockSpec((TM, V_HD), lambda i: (jnp.where(i < N_TILES_P, i % SEQ_TILES, SEQ_TILES), 0))
    fin_spec = pl.BlockSpec((None, TM * N_HEADS, V_HD), lambda i: (l, jnp.minimum(i, N_TILES_P - 1), 0))
    dec_spec = lambda w: pl.BlockSpec((TM * N_HEADS, w), lambda i: (0, 0))
    act = lambda dt: jax.ShapeDtypeStruct((N_PAD, LRU_W), dt)
    act_t = jax.ShapeDtypeStruct((LRU_W, N_PAD), bf16)
    fin = jax.ShapeDtypeStruct((DEPTH, N_PROMPT * N_HEADS, V_HD), f32)
    dec = jax.ShapeDtypeStruct((TM * N_HEADS, V_HD), f32)
    in_specs = [x_spec, _const_spec(x_dec.shape),
                _layer_spec((1, D_MODEL), (l, 0)), _layer_spec((1, D_MODEL), (l, 1)),
                _layer_spec((D_MODEL, D_FF), (l, 0)), _layer_spec((D_MODEL, D_FF), (l, 0)),
                _layer_spec((D_FF, D_MODEL), (l, 0)), _layer_spec((D_MODEL, 5 * LRU_W), (l,)),
                tab_spec, tab_spec, tab_spec]
    args = [x, x_dec, g_norm3, g_norm3, wg, wu, wd, win, *tabs]
    aliases = {}
    if kfin is not None:
        aliases = {len(args): 7, len(args) + 1: 8}
        in_specs += [pl.BlockSpec(memory_space=pl.ANY)] * 2
        args += [kfin, vfin]
    return pl.pallas_call(
        functools.partial(_x_kernel, split_input=split_input),
        grid=(N_TILES,),
        in_specs=in_specs,
        out_specs=[tile(D_MODEL), tile(LRU_W), tile(LRU_W),
                   pl.BlockSpec((TM, ATT_W), lambda i: (0, 0)), tile_t, tile(ATT_W), tile_t,
                   fin_spec, fin_spec, dec_spec(V_HD), dec_spec(V_HD)],
        out_shape=[jax.ShapeDtypeStruct((N_PAD, D_MODEL), f32), act(f32), act(f32),
                   jax.ShapeDtypeStruct((TM, ATT_W), bf16), act_t, act(bf16), act_t,
                   fin, fin, dec, dec],
        input_output_aliases=aliases,
        compiler_params=pltpu.CompilerParams(dimension_semantics=("arbitrary",),
                                             vmem_limit_bytes=VMEM_LIMIT),
        name=f"ffn1_inproj_{l}",
    )(*args)


def _y_kernel(x_ref, ol_ref, oa_ref, p_ref, pdec_ref, wo_ref, gn2_ref, wg_ref, wu_ref, wd_ref,
              gn3_ref, wpg_ref, wpe_ref, *rest, final):
    x = x_ref[...]
    x2 = (x + jnp.dot(ol_ref[...], wo_ref[:LRU_W, :], preferred_element_type=f32)
          + lax.dot_general(oa_ref[...], wo_ref[LRU_W:, :], (((0,), (0,)), ((), ())),
                            preferred_element_type=f32))
    h = _rms(x2, gn2_ref[...]).astype(bf16)
    x3 = x2 + 0.5 * _swiglu(h, wg_ref, wu_ref, wd_ref)
    h3 = _rms(x3, gn3_ref[...]).astype(bf16)
    gate = jax.nn.sigmoid(jnp.dot(h3, wpg_ref[...], preferred_element_type=f32))
    p = _token_tile(pl.program_id(0), p_ref, pdec_ref)
    pe = jnp.dot(p.astype(bf16), wpe_ref[...], preferred_element_type=f32)
    x4 = x3 + gate * pe
    if final:
        gf_ref, yp_ref, ys_ref = rest
        y = _rms(x4, gf_ref[...])
        i = pl.program_id(0)

        @pl.when(i < N_TILES_P)
        def _():
            yp_ref[...] = y

        @pl.when(i == N_TILES_P)
        def _():
            ys_ref[...] = y
    else:
        (xo_ref,) = rest
        xo_ref[...] = x4


def _y_call(l, x, o_lru, o_attn, p_prompt, p_dec, wo, g_norm3, wg, wu, wd, wpg, wpe, g_final):
    final = l == DEPTH - 1
    tile = lambda w: pl.BlockSpec((TM, w), lambda i: (i, 0))
    in_specs = [tile(D_MODEL), tile(LRU_W), pl.BlockSpec((ATT_W, TM), lambda i: (0, i)),
                pl.BlockSpec((None, TM, P_DIM), lambda i: (l, jnp.minimum(i, N_TILES_P - 1), 0)),
                pl.BlockSpec((None, DEC_BATCH, P_DIM), lambda i: (l, 0, 0)),
                _layer_spec((D_MODEL, D_MODEL), (l,)),
                _layer_spec((1, D_MODEL), (l, 2)),
                _layer_spec((D_MODEL, D_FF), (l, 1)), _layer_spec((D_MODEL, D_FF), (l, 1)),
                _layer_spec((D_FF, D_MODEL), (l, 1)),
                _layer_spec((1, D_MODEL), (l, 3)),
                _layer_spec((D_MODEL, D_MODEL), (l,)), _layer_spec((P_DIM, D_MODEL), (l,))]
    args = [x, o_lru, o_attn, p_prompt, p_dec, wo, g_norm3, wg, wu, wd, g_norm3, wpg, wpe]
    out_specs = tile(D_MODEL)
    out_shape = jax.ShapeDtypeStruct((N_PAD, D_MODEL), f32)
    if final:
        in_specs.append(_const_spec((1, D_MODEL)))
        args.append(g_final)
        out_specs = [pl.BlockSpec((TM, D_MODEL), lambda i: (jnp.minimum(i, N_TILES_P - 1), 0)),
                     pl.BlockSpec((TM, D_MODEL), lambda i: (0, 0))]
        out_shape = [jax.ShapeDtypeStruct((N_PROMPT, D_MODEL), f32),
                     jax.ShapeDtypeStruct((TM, D_MODEL), f32)]
    return pl.pallas_call(
        functools.partial(_y_kernel, final=final),
        grid=(N_TILES,),
        in_specs=in_specs,
        out_specs=out_specs,
        out_shape=out_shape,
        compiler_params=pltpu.CompilerParams(dimension_semantics=("arbitrary",),
                                             vmem_limit_bytes=VMEM_LIMIT),
        name=f"outproj_ffn2_embed_{l}",
    )(*args)


def _softplus(x):
    return jnp.maximum(x, 0.0) + jnp.log1p(jnp.exp(-jnp.abs(x)))


def _lru_gates(xc, wa_ref, wi_ref, ba_ref, bi_ref, ap_ref):
    xb = xc.astype(bf16)
    half = LRU_W // 2

    def bd(w_ref):
        lo = jnp.dot(xb[:, :half], w_ref[0], preferred_element_type=f32)
        hi = jnp.dot(xb[:, half:], w_ref[1], preferred_element_type=f32)
        return jnp.concatenate([lo, hi], axis=-1)

    r = jax.nn.sigmoid(bd(wa_ref) + ba_ref[...])
    i = jax.nn.sigmoid(bd(wi_ref) + bi_ref[...])
    log_a = -C_GATE * r * _softplus(-ap_ref[...])
    a = jnp.exp(log_a)
    b = jnp.sqrt(-jnp.tanh(log_a) * (a * a + 1.0)) * (i * xc)
    return a, b


def _lru_prompt_kernel(u_ref, gate_ref, cw_ref, cb_ref, wa_ref, wi_ref, ba_ref, bi_ref, ap_ref,
                       oinit_ref, o_ref, conv_ref, hlast_ref, ubuf, abuf, bbuf, hcar):
    del oinit_ref
    t = pl.program_id(1)

    @pl.when(t == 0)
    def _():
        ubuf[0:SUB, :] = jnp.zeros((SUB, LRU_W), f32)
        hcar[...] = jnp.zeros((1, LRU_W), f32)

    ubuf[SUB:SUB + TT, :] = u_ref[...]
    xc = cb_ref[...]
    for j in range(CONV_W):
        off = SUB - (CONV_W - 1) + j
        xc = xc + ubuf[off:off + TT, :] * cw_ref[j:j + 1, :]
    conv_ref[...] = ubuf[TT + SUB - (CONV_W - 1):TT + SUB, :]
    ubuf[0:SUB, :] = ubuf[TT:TT + SUB, :]

    a, b = _lru_gates(xc, wa_ref, wi_ref, ba_ref, bi_ref, ap_ref)
    abuf[...] = a
    bbuf[...] = b

    row = lax.broadcasted_iota(jnp.int32, (SUB, LRU_W), 0)

    def group(g, h_prev):
        off = pl.multiple_of(g * SUB, SUB)
        ag = abuf[pl.ds(off, SUB), :]
        bg = bbuf[pl.ds(off, SUB), :]
        for d in (1, 2, 4):
            keep = row >= d
            a_sh = pltpu.roll(ag, d, 0)
            b_sh = pltpu.roll(bg, d, 0)
            bg = jnp.where(keep, ag * b_sh + bg, bg)
            ag = jnp.where(keep, ag * a_sh, ag)
        hg = ag * h_prev + bg
        bbuf[pl.ds(off, SUB), :] = hg
        return hg[SUB - 1:SUB, :]

    h_last = lax.fori_loop(0, TT // SUB, group, hcar[...])
    hcar[...] = h_last
    hlast_ref[...] = h_last
    o_ref[...] = (bbuf[...] * jax.nn.gelu(gate_ref[...])).astype(bf16)


def _lru_prompt_call(l, u, gate, o_init, cw, cb, wa, wi, ba, bi, ap):
    tile = pl.BlockSpec((TT, LRU_W), lambda b, t: (b * (SEQ // TT) + t, 0))
    vec = lambda: pl.BlockSpec((None, 1, LRU_W), lambda b, t: (l, 0, 0))
    return pl.pallas_call(
        _lru_prompt_kernel,
        grid=(BATCH, SEQ // TT),
        in_specs=[tile, tile,
                  pl.BlockSpec((None, CONV_W, LRU_W), lambda b, t: (l, 0, 0)), vec(),
                  pl.BlockSpec((None, 2, LRU_W // 2, LRU_W // 2), lambda b, t: (l, 0, 0, 0)),
                  pl.BlockSpec((None, 2, LRU_W // 2, LRU_W // 2), lambda b, t: (l, 0, 0, 0)),
                  vec(), vec(), vec(),
                  pl.BlockSpec(memory_space=pl.ANY)],
        out_specs=[tile,
                   pl.BlockSpec((None, CONV_W - 1, LRU_W), lambda b, t: (b, 0, 0)),
                   pl.BlockSpec((None, 1, LRU_W), lambda b, t: (b, 0, 0))],
        out_shape=[jax.ShapeDtypeStruct((N_PAD, LRU_W), bf16),
                   jax.ShapeDtypeStruct((BATCH, CONV_W - 1, LRU_W), f32),
                   jax.ShapeDtypeStruct((BATCH, 1, LRU_W), f32)],
        scratch_shapes=[pltpu.VMEM((TT + SUB, LRU_W), f32), pltpu.VMEM((TT, LRU_W), f32),
                        pltpu.VMEM((TT, LRU_W), f32), pltpu.VMEM((1, LRU_W), f32)],
        input_output_aliases={9: 0},
        compiler_params=pltpu.CompilerParams(dimension_semantics=("arbitrary", "arbitrary"),
                                             vmem_limit_bytes=VMEM_LIMIT),
        name=f"rglru_prompt_{l}",
    )(u, gate, cw, cb, wa, wi, ba, bi, ap, o_init)


def _lru_decode_kernel(u_ref, gate_ref, sc_ref, h0_ref, cw_ref, cb_ref, wa_ref, wi_ref,
                       ba_ref, bi_ref, ap_ref, o_ref, conv_ref, h_ref):
    u = u_ref[...]
    xc = cb_ref[...]
    for j in range(CONV_W - 1):
        xc = xc + sc_ref[j] * cw_ref[j:j + 1, :]
    xc = xc + u * cw_ref[CONV_W - 1:CONV_W, :]
    a, b = _lru_gates(xc, wa_ref, wi_ref, ba_ref, bi_ref, ap_ref)
    h = a * h0_ref[...] + b
    h_ref[...] = h
    for j in range(CONV_W - 2):
        conv_ref[j] = sc_ref[j + 1]
    conv_ref[CONV_W - 2] = u
    o_ref[...] = (h * jax.nn.gelu(gate_ref[...])).astype(bf16)


def _lru_decode_call(l, u, gate, sc_t, h0, cw, cb, wa, wi, ba, bi, ap):
    rows = pl.BlockSpec((DEC_BATCH, LRU_W), lambda i: (N_PROMPT // DEC_BATCH, 0))
    vec = lambda: pl.BlockSpec((None, 1, LRU_W), lambda i: (l, 0, 0))
    return pl.pallas_call(
        _lru_decode_kernel,
        grid=(1,),
        in_specs=[rows, rows,
                  pl.BlockSpec((None, CONV_W - 1, DEC_BATCH, LRU_W), lambda i: (l, 0, 0, 0)),
                  pl.BlockSpec((None, DEC_BATCH, LRU_W), lambda i: (l, 0, 0)),
                  pl.BlockSpec((None, CONV_W, LRU_W), lambda i: (l, 0, 0)), vec(),
                  pl.BlockSpec((None, 2, LRU_W // 2, LRU_W // 2), lambda i: (l, 0, 0, 0)),
                  pl.BlockSpec((None, 2, LRU_W // 2, LRU_W // 2), lambda i: (l, 0, 0, 0)),
                  vec(), vec(), vec()],
        out_specs=[pl.BlockSpec((DEC_BATCH, LRU_W), lambda i: (0, 0)),
                   pl.BlockSpec((CONV_W - 1, DEC_BATCH, LRU_W), lambda i: (0, 0, 0)),
                   pl.BlockSpec((DEC_BATCH, LRU_W), lambda i: (0, 0))],
        out_shape=[jax.ShapeDtypeStruct((DEC_BATCH, LRU_W), bf16),
                   jax.ShapeDtypeStruct((CONV_W - 1, DEC_BATCH, LRU_W), f32),
                   jax.ShapeDtypeStruct((DEC_BATCH, LRU_W), f32)],
        name=f"rglru_decode_{l}",
    )(u, gate, sc_t, h0, cw, cb, wa, wi, ba, bi, ap)


def _subln(o, gs, lam_init):
    ms = jnp.mean(o * o, axis=-1, keepdims=True)
    return (o * lax.rsqrt(ms + EPS) * gs) * (1.0 - lam_init)


def _attn_prompt_kernel(qt_ref, k_ref, vt_ref, wl_ref, gsb_ref, oinit_ref, o_ref,
                        s_buf0, s_buf1, m1, a1, m2, a2, *, lam_init):
    del oinit_ref
    qi = pl.program_id(2)
    q = qt_ref[...].astype(f32)
    comp_row = lax.broadcasted_iota(jnp.int32, (V_HD, TQ), 0)
    qs = (jnp.where(comp_row < QK_HD, q, 0.0).astype(bf16),
          jnp.where(comp_row >= QK_HD, q, 0.0).astype(bf16))
    stats = ((m1, a1), (m2, a2))
    for m_ref, a_ref in stats:
        m_ref[...] = jnp.full((1, TQ), NEG, f32)
        a_ref[...] = jnp.zeros((V_HD + SUB, TQ), f32)
    ones = jnp.ones((SUB, TQ), bf16)

    def scores(off, s_ref, masked):
        kb = k_ref[pl.ds(pl.multiple_of(off, TQ), TQ), :]
        for c in range(2):
            s = jnp.dot(kb, qs[c], preferred_element_type=f32)
            if masked:
                key = lax.broadcasted_iota(jnp.int32, (TQ, TQ), 0)
                qry = lax.broadcasted_iota(jnp.int32, (TQ, TQ), 1)
                s = jnp.where(key <= qry, s, NEG)
            s_ref[c] = s

    def consume(off, s_ref):
        vt = jnp.concatenate([vt_ref[:, pl.ds(pl.multiple_of(off, TQ), TQ)], ones], axis=0)
        for c, (m_ref, a_ref) in enumerate(stats):
            s = s_ref[c]
            m_old = m_ref[...]
            m_new = jnp.maximum(m_old, jnp.max(s, axis=0, keepdims=True))
            alpha = jnp.exp2(m_old - m_new)
            p = jnp.exp2(s - m_new).astype(bf16)
            a_ref[...] = alpha * a_ref[...] + jnp.dot(vt, p, preferred_element_type=f32)
            m_ref[...] = m_new

    diag = qi * TQ
    scores(diag, s_buf0, True)

    def pair(u, prev_off):
        off1 = 2 * u * TQ
        scores(off1, s_buf1, False)
        consume(prev_off, s_buf0)
        off2 = off1 + TQ
        scores(off2, s_buf0, False)
        consume(off1, s_buf1)
        return off2

    last_off = lax.fori_loop(0, qi // 2, pair, diag)

    @pl.when(qi % 2 == 1)
    def _():
        off = (qi - 1) * TQ
        scores(off, s_buf1, False)
        consume(last_off, s_buf0)
        consume(off, s_buf1)

    @pl.when(qi % 2 == 0)
    def _():
        consume(last_off, s_buf0)

    lam = _lam(wl_ref, lam_init)
    inv1 = 1.0 / a1[V_HD:V_HD + 1, :]
    inv2 = 1.0 / a2[V_HD:V_HD + 1, :]
    o = a1[:V_HD, :] * inv1 - lam * (a2[:V_HD, :] * inv2)
    ms = jnp.mean(o * o, axis=0, keepdims=True)
    y = (o * lax.rsqrt(ms + EPS) * gsb_ref[...]) * (1.0 - lam_init)
    o_ref[...] = y.astype(bf16)


def _attn_prompt_call(l, qt, kb, vt, o_init, w_lambda, gsb):
    lam_init = 0.8 - 0.6 * math.exp(-0.3 * l)
    nq = SEQ // TQ
    qo_spec = pl.BlockSpec((V_HD, TQ), lambda b, h, i: (h, b * nq + i))
    return pl.pallas_call(
        functools.partial(_attn_prompt_kernel, lam_init=lam_init),
        grid=(BATCH, N_HEADS, nq),
        in_specs=[qo_spec,
                  pl.BlockSpec((SEQ, V_HD), lambda b, h, i: (b, h)),
                  pl.BlockSpec((V_HD, SEQ), lambda b, h, i: (h, b)),
                  pl.BlockSpec((None, 4, QK_HD), lambda b, h, i: (l, 0, 0)),
                  pl.BlockSpec((None, V_HD, TQ), lambda b, h, i: (l, 0, 0)),
                  pl.BlockSpec(memory_space=pl.ANY)],
        out_specs=qo_spec,
        out_shape=jax.ShapeDtypeStruct((ATT_W, N_PAD), bf16),
        scratch_shapes=[pltpu.VMEM((2, TQ, TQ), f32), pltpu.VMEM((2, TQ, TQ), f32),
                        pltpu.VMEM((1, TQ), f32), pltpu.VMEM((V_HD + SUB, TQ), f32),
                        pltpu.VMEM((1, TQ), f32), pltpu.VMEM((V_HD + SUB, TQ), f32)],
        input_output_aliases={5: 0},
        compiler_params=pltpu.CompilerParams(
            dimension_semantics=("arbitrary", "arbitrary", "arbitrary"),
            vmem_limit_bytes=VMEM_LIMIT),
        name=f"attn_prompt_{l}",
    )(qt, kb, vt, w_lambda, gsb, o_init)


def _attn_decode_kernel(pt_ref, q_ref, kn_ref, vn_ref, wl_ref, gs_ref, *rest, lam_init):
    del pt_ref
    k_refs = rest[:N_PAGES]
    v_refs = rest[N_PAGES:2 * N_PAGES]
    o_ref, e_s, enew_s, coef_s = rest[2 * N_PAGES:]

    @pl.when(pl.program_id(0) == 0)
    def _():
        e_s[...] = jnp.zeros_like(e_s)
        enew_s[...] = jnp.zeros_like(enew_s)
        coef_s[...] = jnp.zeros_like(coef_s)

    o = _decode_values(e_s[...], enew_s[...], coef_s[...], vn_ref[...], v_refs)
    o_ref[...] = _subln(o, gs_ref[...], lam_init)
    e, e_new, coef = _decode_scores(q_ref[...].astype(f32), kn_ref[...], k_refs, _lam(wl_ref, lam_init))
    e_s[...] = e
    enew_s[...] = e_new
    coef_s[...] = coef


_NT = (((1,), (1,)), ((), ()))
_NC = 2 * N_HEADS


def _decode_scores(q, kn, k_refs, lam):
    nc = _NC
    nt = _NT
    row8 = lax.broadcasted_iota(jnp.int32, (nc, V_HD), 0)
    lane8 = lax.broadcasted_iota(jnp.int32, (nc, V_HD), 1)
    lane1 = lax.broadcasted_iota(jnp.int32, (1, V_HD), 1)
    q8 = jnp.where(jnp.right_shift(lane8, 6) == jnp.bitwise_and(row8, 1), q, 0.0)
    q8 = q8 * QK_SCALE
    q_rows = jnp.concatenate([q8] * (V_HD // nc), axis=0)
    q_rows = jnp.concatenate([q_rows] * N_PAGES, axis=1)
    rq = lax.broadcasted_iota(jnp.int32, (V_HD, N_PAGES * V_HD), 0)
    lq = lax.broadcasted_iota(jnp.int32, (V_HD, N_PAGES * V_HD), 1)
    qmat = jnp.where(jnp.right_shift(rq, 3) == jnp.right_shift(lq, 7), q_rows, 0.0).astype(bf16)

    half = N_PAGES // 2
    s = None
    for g in range(2):
        kcat = jnp.concatenate([k_refs[p][...].astype(bf16) for p in range(g * half, (g + 1) * half)],
                               axis=1)
        sg = lax.dot_general(kcat, qmat[:, g * half * V_HD:(g + 1) * half * V_HD], nt,
                             preferred_element_type=f32)
        s = sg if s is None else s + sg
    rowp = lax.broadcasted_iota(jnp.int32, (PAGE_ROWS, V_HD), 0)
    colp = lax.broadcasted_iota(jnp.int32, (PAGE_ROWS, V_HD), 1)
    own = jnp.bitwise_and(rowp, N_HEADS - 1) == jnp.right_shift(jnp.bitwise_and(colp, nc - 1), 1)
    s = jnp.where(own, s, NEG)
    s_new = lax.dot_general(kn.astype(bf16), qmat[:, :V_HD], nt, preferred_element_type=f32)
    own8 = (jnp.where(row8 < N_HEADS, row8, -1)
            == jnp.where(lane8 < nc, jnp.right_shift(lane8, 1), -2))
    s_new = jnp.where(own8, s_new, NEG)

    def over_pages(x, op):
        x = jnp.broadcast_to(x, (SUB, V_HD))
        for sh in (nc, 2 * nc, 4 * nc, 8 * nc):
            x = op(x, pltpu.roll(x, sh, 1))
        return x[0:1, :]

    m = over_pages(jnp.maximum(jnp.max(s, axis=0, keepdims=True),
                               jnp.max(s_new, axis=0, keepdims=True)), jnp.maximum)
    e = jnp.exp(s - m)
    e_new = jnp.exp(s_new - m)
    denom = over_pages(jnp.sum(e, axis=0, keepdims=True) + jnp.sum(e_new, axis=0, keepdims=True),
                       jnp.add)
    coef = jnp.where(jnp.bitwise_and(lane1, 1) == 0, 1.0, -lam) / denom
    return e.astype(bf16), e_new.astype(bf16), coef


def _decode_values(e, e_new, coef, vn, v_refs):
    nc = _NC
    lane1 = lax.broadcasted_iota(jnp.int32, (1, V_HD), 1)

    def lane_weights(first_lane):
        sel = jnp.where(jnp.right_shift(lane1, 3) == first_lane // nc, coef, 0.0)
        return jnp.broadcast_to(sel, (V_HD, V_HD)).astype(bf16)

    w_new = lax.dot_general(e_new, lane_weights(0), _NT, preferred_element_type=f32)
    acc = w_new * vn
    for p in range(0, N_PAGES, 2):
        both = jnp.concatenate([lane_weights(nc * p), lane_weights(nc * (p + 1))], axis=0)
        w = lax.dot_general(e, both, _NT, preferred_element_type=f32)
        wv = w[:, :V_HD] * v_refs[p][...] + w[:, V_HD:] * v_refs[p + 1][...]
        acc = acc + jnp.sum(wv.reshape(PAGE_ROWS // SUB, SUB, V_HD), axis=0)
    return acc[:N_HEADS, :] + acc[N_HEADS:, :]


def _attn_decode_call(l, page_table, q_rep, kn8, vn8, cache_k, cache_v, w_lambda, g_subln3):
    lam_init = 0.8 - 0.6 * math.exp(-0.3 * l)
    last = DEC_BATCH - 1
    cur = lambda t: jnp.minimum(t, last)
    prev = lambda t: jnp.maximum(t - 1, 0)
    row_spec = lambda tok: pl.BlockSpec((None, _NC, V_HD), lambda t, pt: (tok(t), 0, 0))

    def page_spec(tok, p):
        return pl.BlockSpec((None, None, PAGE_ROWS, V_HD), lambda t, pt: (l, pt[tok(t), p], 0, 0))

    grid_spec = pltpu.PrefetchScalarGridSpec(
        num_scalar_prefetch=1,
        grid=(DEC_BATCH + 1,),
        in_specs=[row_spec(cur), row_spec(cur), row_spec(prev),
                  pl.BlockSpec((None, 4, QK_HD), lambda t, pt: (l, 0, 0)),
                  pl.BlockSpec((None, 1, V_HD), lambda t, pt: (l, 0, 0))]
                 + [page_spec(cur, p) for p in range(N_PAGES)]
                 + [page_spec(prev, p) for p in range(N_PAGES)],
        out_specs=pl.BlockSpec((None, N_HEADS, V_HD), lambda t, pt: (prev(t), 0, 0)),
        scratch_shapes=[pltpu.VMEM((PAGE_ROWS, V_HD), bf16), pltpu.VMEM((_NC, V_HD), bf16),
                        pltpu.VMEM((1, V_HD), f32)],
    )
    return pl.pallas_call(
        functools.partial(_attn_decode_kernel, lam_init=lam_init),
        grid_spec=grid_spec,
        out_shape=jax.ShapeDtypeStruct((DEC_BATCH, N_HEADS, V_HD), f32),
        compiler_params=pltpu.CompilerParams(dimension_semantics=("arbitrary",),
                                             vmem_limit_bytes=VMEM_LIMIT),
        name=f"attn_decode_{l}",
    )(page_table, q_rep, kn8, vn8, w_lambda, g_subln3,
      *([cache_k] * N_PAGES), *([cache_v] * N_PAGES))


def _block_diag(w):
    per = LRU_BLOCKS // 2
    w = w.reshape(DEPTH, 2, per, LRU_BS, LRU_BS)
    eye = jnp.eye(per, dtype=w.dtype)
    return jnp.einsum('dgnij,nm->dgnimj', w, eye).reshape(DEPTH, 2, per * LRU_BS, per * LRU_BS)


@jax.jit
def kernel(x_prompt, x_sample, cache_k, cache_v, page_table, state_conv, state_h, p_prompt, p_sample,
           w_in, w_out, conv_w, conv_b, w_a, b_a, w_i, b_i, a_param, w_lambda, g_subln,
           w_ffn_gate, w_ffn_up, w_ffn_down, w_pe, w_pg, g_norm, g_final):
    n_dec = DEC_BATCH
    x = x_prompt.reshape(N_PROMPT, D_MODEL)
    x_dec = x_sample.reshape(n_dec, D_MODEL)
    p_pr = p_prompt.reshape(DEPTH, N_PROMPT, P_DIM)
    p_dec = p_sample.reshape(DEPTH, n_dec, P_DIM)

    wg_b = w_ffn_gate.astype(bf16)
    wu_b = w_ffn_up.astype(bf16)
    wd_b = w_ffn_down.astype(bf16)
    win_b = w_in.astype(bf16)
    wo_b = w_out.astype(bf16)
    wpg_b = w_pg.astype(bf16)
    wpe_b = w_pe.astype(bf16)
    wa_b = _block_diag(w_a).astype(bf16)
    wi_b = _block_diag(w_i).astype(bf16)
    g_norm3 = g_norm.reshape(DEPTH, 4, 1, D_MODEL)
    g_final2 = g_final.reshape(1, D_MODEL)
    g_subln3 = g_subln.reshape(DEPTH, 1, V_HD)
    cb3 = conv_b.reshape(DEPTH, 1, LRU_W)
    ba3 = b_a.reshape(DEPTH, 1, LRU_W)
    bi3 = b_i.reshape(DEPTH, 1, LRU_W)
    ap3 = a_param.reshape(DEPTH, 1, LRU_W)
    sc_t = state_conv.transpose(0, 2, 1, 3)
    ck = cache_k.reshape(DEPTH, -1, PAGE_ROWS, V_HD)
    cv = cache_v.reshape(DEPTH, -1, PAGE_ROWS, V_HD)
    gsb = jnp.broadcast_to(g_subln[:, :, None], (DEPTH, V_HD, TQ))
    tabs = _rope_tables()

    ks_s, vs_s, convs_p, hs_p, convs_s, hs_s = [], [], [], [], [], []
    kfin = vfin = None
    dec_rows = n_dec * N_HEADS
    for l in range(DEPTH):
        x, u, gate, qs, qt, kb, vt, kfin, vfin, ksn, vsn = _x_call(
            l, x, x_dec, g_norm3, wg_b, wu_b, wd_b, win_b, tabs, kfin, vfin)
        k_new = ksn[:dec_rows].reshape(n_dec, N_HEADS, V_HD)
        v_new = vsn[:dec_rows].reshape(n_dec, N_HEADS, V_HD)
        ks_s.append(k_new)
        vs_s.append(v_new)

        o_lru, conv_p, h_p = _lru_prompt_call(l, u, gate, jnp.zeros((N_PAD, LRU_W), bf16),
                                              conv_w, cb3, wa_b, wi_b, ba3, bi3, ap3)
        o_lru_s, conv_s, h_s = _lru_decode_call(l, u, gate, sc_t, state_h,
                                                conv_w, cb3, wa_b, wi_b, ba3, bi3, ap3)
        o_lru = lax.dynamic_update_slice(o_lru, o_lru_s, (N_PROMPT, 0))
        convs_p.append(conv_p)
        hs_p.append(h_p.reshape(BATCH, LRU_W))
        convs_s.append(conv_s.transpose(1, 0, 2))
        hs_s.append(h_s)

        o_attn_t = _attn_prompt_call(l, qt, kb, vt, jnp.zeros((ATT_W, N_PAD), bf16), w_lambda, gsb)
        q_rep = jnp.repeat(qs[:n_dec].reshape(n_dec, N_HEADS, V_HD), 2, axis=1)
        pad8 = ((0, 0), (0, 2 * N_HEADS - N_HEADS), (0, 0))
        o_attn_s = _attn_decode_call(l, page_table, q_rep, jnp.pad(k_new, pad8), jnp.pad(v_new, pad8),
                                     ck, cv, w_lambda, g_subln3)
        o_attn_t = lax.dynamic_update_slice(
            o_attn_t, o_attn_s.reshape(n_dec, ATT_W).astype(bf16).T, (0, N_PROMPT))

        x = _y_call(l, x, o_lru, o_attn_t, p_pr, p_dec, wo_b, g_norm3, wg_b, wu_b, wd_b, wpg_b, wpe_b,
                    g_final2)
    y_p, y_s = x
    y_prompt = y_p.reshape(BATCH, SEQ, D_MODEL)
    y_sample = y_s[:n_dec].reshape(n_dec, 1, D_MODEL)
    k_prompt = kfin.reshape(DEPTH, BATCH, SEQ, N_HEADS, V_HD)
    v_prompt = vfin.reshape(DEPTH, BATCH, SEQ, N_HEADS, V_HD)
    k_sample = jnp.stack(ks_s).reshape(DEPTH, n_dec, 1, N_HEADS, V_HD)
    v_sample = jnp.stack(vs_s).reshape(DEPTH, n_dec, 1, N_HEADS, V_HD)
    return (y_prompt, y_sample, k_prompt, v_prompt, jnp.stack(convs_p), jnp.stack(hs_p),
            k_sample, v_sample, jnp.stack(convs_s), jnp.stack(hs_s))
```

```python
import functools
import math

import jax
import jax.numpy as jnp
from jax import lax
from jax.experimental import pallas as pl
from jax.experimental.pallas import tpu as pltpu

f32 = jnp.float32
bf16 = jnp.bfloat16

D_MODEL = 1024
BATCH = 4
SEQ = 4096
DEPTH = 4
DEC_BATCH = 128
PAST_LEN = 2048
PAGE_SIZE = 128
N_PAGES = PAST_LEN // PAGE_SIZE
N_HEADS = 4
PAGE_ROWS = PAGE_SIZE * N_HEADS
LRU_W = 512
LRU_BLOCKS = 8
LRU_BS = 64
CONV_W = 4
C_GATE = 8.0
ATT_W = 512
QK_HD = 64
V_HD = 128
ROT_DIM = 16
ROPE_THETA = 500000.0
D_FF = 2816
P_DIM = 256
EPS = 1e-6
NEG = -1e30
QK_SCALE = QK_HD ** -0.5
LOG2E = math.log2(math.e)

N_PROMPT = BATCH * SEQ
TM = 512
N_TILES_P = N_PROMPT // TM
N_TILES = N_TILES_P + 1
N_PAD = N_TILES * TM
SEQ_TILES = SEQ // TM
MXU_COLS = 256
FF_SPLIT = (D_FF // MXU_COLS + 1) // 2 * MXU_COLS
FF_CHUNKS = ((0, FF_SPLIT), (FF_SPLIT, D_FF))
TQ = 512
TT = 512
SUB = 8
VMEM_LIMIT = 56 * 1024 * 1024


def _const_spec(shape):
    nd = len(shape)
    return pl.BlockSpec(shape, lambda *_: (0,) * nd, pipeline_mode=pl.Buffered(1))


def _layer_spec(shape, lead):
    nl = len(lead)
    nd = len(shape)
    return pl.BlockSpec((None,) * nl + tuple(shape), lambda *_: tuple(lead) + (0,) * nd,
                        pipeline_mode=pl.Buffered(1))


def _rms(x, g):
    ms = jnp.mean(x * x, axis=-1, keepdims=True)
    return x * lax.rsqrt(ms + EPS) * g


def _swiglu(hb, wg_ref, wu_ref, wd_ref):
    acc = None
    for lo, hi in FF_CHUNKS:
        sl = slice(lo, hi)
        g = jnp.dot(hb, wg_ref[:, sl], preferred_element_type=f32)
        u = jnp.dot(hb, wu_ref[:, sl], preferred_element_type=f32)
        a = (g * jax.nn.sigmoid(g) * u).astype(bf16)
        y = jnp.dot(a, wd_ref[sl, :], preferred_element_type=f32)
        acc = y if acc is None else acc + y
    return acc


def _lam(wl_ref, lam_init):
    wl = wl_ref[...]
    s01 = jnp.sum(wl[0:1, :] * wl[1:2, :], axis=-1, keepdims=True)
    s23 = jnp.sum(wl[2:3, :] * wl[3:4, :], axis=-1, keepdims=True)
    return jnp.exp(s01) - jnp.exp(s23) + lam_init


def _rope_table_kernel(freq_ref, c_ref, sa_ref, sb_ref):
    rows = c_ref.shape[0]
    row = lax.broadcasted_iota(jnp.int32, (rows, V_HD), 0)
    lane = lax.broadcasted_iota(jnp.int32, (rows, V_HD), 1)
    pos = jnp.where(row < SEQ, row, PAST_LEN).astype(f32)
    ang = pos * freq_ref[...]
    cos = jnp.cos(ang)
    sin = jnp.sin(ang)
    in_comp = lane % QK_HD
    half = ROT_DIM // 2
    c_ref[...] = jnp.where(in_comp < ROT_DIM, cos, 1.0)
    sa_ref[...] = jnp.where(in_comp < half, -sin, 0.0)
    sb_ref[...] = jnp.where((in_comp >= half) & (in_comp < ROT_DIM), sin, 0.0)


def _rope_tables():
    half = ROT_DIM // 2
    freqs = jnp.power(jnp.float32(ROPE_THETA), -jnp.arange(0, ROT_DIM, 2, dtype=f32) / ROT_DIM)
    lane = jnp.arange(V_HD)
    freq_lane = freqs[(lane % QK_HD) % half].reshape(1, V_HD)
    rows = SEQ + TM
    shp = jax.ShapeDtypeStruct((rows, V_HD), f32)
    return pl.pallas_call(
        _rope_table_kernel,
        out_shape=(shp, shp, shp),
        name="rope_tables",
    )(freq_lane)


def _store_heads_interleaved(dst_ref, t):
    for hd in range(N_HEADS):
        dst_ref[pl.ds(hd, TM, stride=N_HEADS), :] = t[:, hd * V_HD:(hd + 1) * V_HD]


def _token_tile(i, prompt_ref, dec_ref, axis=0):
    dec = dec_ref[...]
    pad_shape = list(dec.shape)
    pad_shape[axis] = TM - dec.shape[axis]
    dec_tile = jnp.concatenate([dec, jnp.zeros(pad_shape, dec.dtype)], axis=axis)
    return jnp.where(i < N_TILES_P, prompt_ref[...], dec_tile)


def _x_kernel(x_ref, xdec_ref, gn0_ref, gn1_ref, wg_ref, wu_ref, wd_ref, win_ref, c_ref, sa_ref, sb_ref,
              *rest, split_input):
    (xo_ref, u_ref, gate_ref, qs_ref, qt_ref, kb_ref, vt_ref,
     kfin_ref, vfin_ref, ksn_ref, vsn_ref) = rest[-11:]
    i = pl.program_id(0)
    x = _token_tile(i, x_ref, xdec_ref) if split_input else x_ref[...]
    h = _rms(x, gn0_ref[...]).astype(bf16)
    x1 = x + 0.5 * _swiglu(h, wg_ref, wu_ref, wd_ref)
    xo_ref[...] = x1
    h1 = _rms(x1, gn1_ref[...]).astype(bf16)

    def proj(i):
        return jnp.dot(h1, win_ref[:, i * LRU_W:(i + 1) * LRU_W], preferred_element_type=f32)

    u_ref[...] = proj(0)
    gate_ref[...] = proj(1)
    cos = c_ref[...]
    sa = sa_ref[...]
    sb = sb_ref[...]

    def rope(t):
        outs = []
        for hd in range(N_HEADS):
            th = t[:, hd * V_HD:(hd + 1) * V_HD]
            up = pltpu.roll(th, V_HD - ROT_DIM // 2, 1)
            dn = pltpu.roll(th, ROT_DIM // 2, 1)
            outs.append(th * cos + up * sa + dn * sb)
        return jnp.concatenate(outs, axis=-1)

    q = rope(proj(2))
    qt_ref[...] = (q * (QK_SCALE * LOG2E)).T.astype(bf16)
    k = rope(proj(3))
    kb_ref[...] = k.astype(bf16)
    v = proj(4)
    vt_ref[...] = v.T.astype(bf16)

    @pl.when(i < N_TILES_P)
    def _():
        _store_heads_interleaved(kfin_ref, k)
        _store_heads_interleaved(vfin_ref, v)

    @pl.when(i == N_TILES_P)
    def _():
        qs_ref[...] = q.astype(bf16)
        _store_heads_interleaved(ksn_ref, k)
        _store_heads_interleaved(vsn_ref, v)


def _x_call(l, x, x_dec, g_norm3, wg, wu, wd, win, tabs, kfin, vfin):
    split_input = x.shape[0] == N_PROMPT
    tile = lambda w: pl.BlockSpec((TM, w), lambda i: (i, 0))
    x_spec = (pl.BlockSpec((TM, D_MODEL), lambda i: (jnp.minimum(i, N_TILES_P - 1), 0))
              if split_input else tile(D_MODEL))
    tile_t = pl.BlockSpec((LRU_W, TM), lambda i: (0, i))
    tab_spec = pl.BlockSpec((TM, V_HD), lambda i: (jnp.where(i < N_TILES_P, i % SEQ_TILES, SEQ_TILES), 0))
    fin_spec = pl.BlockSpec((None, TM * N_HEADS, V_HD), lambda i: (l, jnp.minimum(i, N_TILES_P - 1), 0))
    dec_spec = lambda w: pl.BlockSpec((TM * N_HEADS, w), lambda i: (0, 0))
    act = lambda dt: jax.ShapeDtypeStruct((N_PAD, LRU_W), dt)
    act_t = jax.ShapeDtypeStruct((LRU_W, N_PAD), bf16)
    fin = jax.ShapeDtypeStruct((DEPTH, N_PROMPT * N_HEADS, V_HD), f32)
    dec = jax.ShapeDtypeStruct((TM * N_HEADS, V_HD), f32)
    in_specs = [x_spec, _const_spec(x_dec.shape),
                _layer_spec((1, D_MODEL), (l, 0)), _layer_spec((1, D_MODEL), (l, 1)),
                _layer_spec((D_MODEL, D_FF), (l, 0)), _layer_spec((D_MODEL, D_FF), (l, 0)),
                _layer_spec((D_FF, D_MODEL), (l, 0)), _layer_spec((D_MODEL, 5 * LRU_W), (l,)),
                tab_spec, tab_spec, tab_spec]
    args = [x, x_dec, g_norm3, g_norm3, wg, wu, wd, win, *tabs]
    aliases = {}
    if kfin is not None:
        aliases = {len(args): 7, len(args) + 1: 8}
        in_specs += [pl.BlockSpec(memory_space=pl.ANY)] * 2
        args += [kfin, vfin]
    return pl.pallas_call(
        functools.partial(_x_kernel, split_input=split_input),
        grid=(N_TILES,),
        in_specs=in_specs,
        out_specs=[tile(D_MODEL), tile(LRU_W), tile(LRU_W),
                   pl.BlockSpec((TM, ATT_W), lambda i: (0, 0)), tile_t, tile(ATT_W), tile_t,
                   fin_spec, fin_spec, dec_spec(V_HD), dec_spec(V_HD)],
        out_shape=[jax.ShapeDtypeStruct((N_PAD, D_MODEL), f32), act(f32), act(f32),
                   jax.ShapeDtypeStruct((TM, ATT_W), bf16), act_t, act(bf16), act_t,
                   fin, fin, dec, dec],
        input_output_aliases=aliases,
        compiler_params=pltpu.CompilerParams(dimension_semantics=("arbitrary",),
                                             vmem_limit_bytes=VMEM_LIMIT),
        name=f"ffn1_inproj_{l}",
    )(*args)


def _y_kernel(x_ref, ol_ref, oldec_ref, oa_ref, oadec_ref, p_ref, pdec_ref, wo_ref, gn2_ref,
              wg_ref, wu_ref, wd_ref, gn3_ref, wpg_ref, wpe_ref, *rest, final):
    i = pl.program_id(0)
    x = x_ref[...]
    o_lru = _token_tile(i, ol_ref, oldec_ref)
    o_attn_t = _token_tile(i, oa_ref, oadec_ref, axis=1)
    x2 = (x + jnp.dot(o_lru, wo_ref[:LRU_W, :], preferred_element_type=f32)
          + lax.dot_general(o_attn_t, wo_ref[LRU_W:, :], (((0,), (0,)), ((), ())),
                            preferred_element_type=f32))
    h = _rms(x2, gn2_ref[...]).astype(bf16)
    x3 = x2 + 0.5 * _swiglu(h, wg_ref, wu_ref, wd_ref)
    h3 = _rms(x3, gn3_ref[...]).astype(bf16)
    gate = jax.nn.sigmoid(jnp.dot(h3, wpg_ref[...], preferred_element_type=f32))
    p = _token_tile(i, p_ref, pdec_ref)
    pe = jnp.dot(p.astype(bf16), wpe_ref[...], preferred_element_type=f32)
    x4 = x3 + gate * pe
    if final:
        gf_ref, yp_ref, ys_ref = rest
        y = _rms(x4, gf_ref[...])

        @pl.when(i < N_TILES_P)
        def _():
            yp_ref[...] = y

        @pl.when(i == N_TILES_P)
        def _():
            ys_ref[...] = y
    else:
        (xo_ref,) = rest
        xo_ref[...] = x4


def _y_call(l, x, o_lru, o_lru_dec, o_attn_t, o_attn_t_dec, p_prompt, p_dec, wo, g_norm3,
            wg, wu, wd, wpg, wpe, g_final):
    final = l == DEPTH - 1
    tile = lambda w: pl.BlockSpec((TM, w), lambda i: (i, 0))
    prompt_tile = lambda i: jnp.minimum(i, N_TILES_P - 1)
    in_specs = [tile(D_MODEL),
                pl.BlockSpec((TM, LRU_W), lambda i: (prompt_tile(i), 0)),
                pl.BlockSpec((DEC_BATCH, LRU_W), lambda i: (0, 0)),
                pl.BlockSpec((ATT_W, TM), lambda i: (0, prompt_tile(i))),
                pl.BlockSpec((ATT_W, DEC_BATCH), lambda i: (0, 0)),
                pl.BlockSpec((None, TM, P_DIM), lambda i: (l, prompt_tile(i), 0)),
                pl.BlockSpec((None, DEC_BATCH, P_DIM), lambda i: (l, 0, 0)),
                _layer_spec((D_MODEL, D_MODEL), (l,)),
                _layer_spec((1, D_MODEL), (l, 2)),
                _layer_spec((D_MODEL, D_FF), (l, 1)), _layer_spec((D_MODEL, D_FF), (l, 1)),
                _layer_spec((D_FF, D_MODEL), (l, 1)),
                _layer_spec((1, D_MODEL), (l, 3)),
                _layer_spec((D_MODEL, D_MODEL), (l,)), _layer_spec((P_DIM, D_MODEL), (l,))]
    args = [x, o_lru, o_lru_dec, o_attn_t, o_attn_t_dec, p_prompt, p_dec,
            wo, g_norm3, wg, wu, wd, g_norm3, wpg, wpe]
    out_specs = tile(D_MODEL)
    out_shape = jax.ShapeDtypeStruct((N_PAD, D_MODEL), f32)
    if final:
        in_specs.append(_const_spec((1, D_MODEL)))
        args.append(g_final)
        out_specs = [pl.BlockSpec((TM, D_MODEL), lambda i: (jnp.minimum(i, N_TILES_P - 1), 0)),
                     pl.BlockSpec((TM, D_MODEL), lambda i: (0, 0))]
        out_shape = [jax.ShapeDtypeStruct((N_PROMPT, D_MODEL), f32),
                     jax.ShapeDtypeStruct((TM, D_MODEL), f32)]
    return pl.pallas_call(
        functools.partial(_y_kernel, final=final),
        grid=(N_TILES,),
        in_specs=in_specs,
        out_specs=out_specs,
        out_shape=out_shape,
        compiler_params=pltpu.CompilerParams(dimension_semantics=("arbitrary",),
                                             vmem_limit_bytes=VMEM_LIMIT),
        name=f"outproj_ffn2_embed_{l}",
    )(*args)


def _softplus(x):
    return jnp.maximum(x, 0.0) + jnp.log1p(jnp.exp(-jnp.abs(x)))


def _lru_gates(xc, wa_ref, wi_ref, ba_ref, bi_ref, ap_ref):
    xb = xc.astype(bf16)
    half = LRU_W // 2

    def bd(w_ref):
        lo = jnp.dot(xb[:, :half], w_ref[0], preferred_element_type=f32)
        hi = jnp.dot(xb[:, half:], w_ref[1], preferred_element_type=f32)
        return jnp.concatenate([lo, hi], axis=-1)

    r = jax.nn.sigmoid(bd(wa_ref) + ba_ref[...])
    i = jax.nn.sigmoid(bd(wi_ref) + bi_ref[...])
    log_a = -C_GATE * r * _softplus(-ap_ref[...])
    a = jnp.exp(log_a)
    b = jnp.sqrt(-jnp.tanh(log_a) * (a * a + 1.0)) * (i * xc)
    return a, b


def _lru_prompt_kernel(u_ref, gate_ref, cw_ref, cb_ref, wa_ref, wi_ref, ba_ref, bi_ref, ap_ref,
                       o_ref, conv_ref, hlast_ref, ubuf, abuf, bbuf, hcar):
    t = pl.program_id(1)

    @pl.when(t == 0)
    def _():
        ubuf[0:SUB, :] = jnp.zeros((SUB, LRU_W), f32)
        hcar[...] = jnp.zeros((1, LRU_W), f32)

    ubuf[SUB:SUB + TT, :] = u_ref[...]
    xc = cb_ref[...]
    for j in range(CONV_W):
        off = SUB - (CONV_W - 1) + j
        xc = xc + ubuf[off:off + TT, :] * cw_ref[j:j + 1, :]
    conv_ref[...] = ubuf[TT + SUB - (CONV_W - 1):TT + SUB, :]
    ubuf[0:SUB, :] = ubuf[TT:TT + SUB, :]

    a, b = _lru_gates(xc, wa_ref, wi_ref, ba_ref, bi_ref, ap_ref)
    abuf[...] = a
    bbuf[...] = b

    row = lax.broadcasted_iota(jnp.int32, (SUB, LRU_W), 0)

    def group(g, h_prev):
        off = pl.multiple_of(g * SUB, SUB)
        ag = abuf[pl.ds(off, SUB), :]
        bg = bbuf[pl.ds(off, SUB), :]
        for d in (1, 2, 4):
            keep = row >= d
            a_sh = pltpu.roll(ag, d, 0)
            b_sh = pltpu.roll(bg, d, 0)
            bg = jnp.where(keep, ag * b_sh + bg, bg)
            ag = jnp.where(keep, ag * a_sh, ag)
        hg = ag * h_prev + bg
        bbuf[pl.ds(off, SUB), :] = hg
        return hg[SUB - 1:SUB, :]

    h_last = lax.fori_loop(0, TT // SUB, group, hcar[...])
    hcar[...] = h_last
    hlast_ref[...] = h_last
    o_ref[...] = (bbuf[...] * jax.nn.gelu(gate_ref[...])).astype(bf16)


def _lru_prompt_call(l, u, gate, cw, cb, wa, wi, ba, bi, ap):
    tile = pl.BlockSpec((TT, LRU_W), lambda b, t: (b * (SEQ // TT) + t, 0))
    vec = lambda: pl.BlockSpec((None, 1, LRU_W), lambda b, t: (l, 0, 0))
    return pl.pallas_call(
        _lru_prompt_kernel,
        grid=(BATCH, SEQ // TT),
        in_specs=[tile, tile,
                  pl.BlockSpec((None, CONV_W, LRU_W), lambda b, t: (l, 0, 0)), vec(),
                  pl.BlockSpec((None, 2, LRU_W // 2, LRU_W // 2), lambda b, t: (l, 0, 0, 0)),
                  pl.BlockSpec((None, 2, LRU_W // 2, LRU_W // 2), lambda b, t: (l, 0, 0, 0)),
                  vec(), vec(), vec()],
        out_specs=[tile,
                   pl.BlockSpec((None, CONV_W - 1, LRU_W), lambda b, t: (b, 0, 0)),
                   pl.BlockSpec((None, 1, LRU_W), lambda b, t: (b, 0, 0))],
        out_shape=[jax.ShapeDtypeStruct((N_PROMPT, LRU_W), bf16),
                   jax.ShapeDtypeStruct((BATCH, CONV_W - 1, LRU_W), f32),
                   jax.ShapeDtypeStruct((BATCH, 1, LRU_W), f32)],
        scratch_shapes=[pltpu.VMEM((TT + SUB, LRU_W), f32), pltpu.VMEM((TT, LRU_W), f32),
                        pltpu.VMEM((TT, LRU_W), f32), pltpu.VMEM((1, LRU_W), f32)],
        compiler_params=pltpu.CompilerParams(dimension_semantics=("arbitrary", "arbitrary"),
                                             vmem_limit_bytes=VMEM_LIMIT),
        name=f"rglru_prompt_{l}",
    )(u, gate, cw, cb, wa, wi, ba, bi, ap)


def _lru_decode_kernel(u_ref, gate_ref, sc_ref, h0_ref, cw_ref, cb_ref, wa_ref, wi_ref,
                       ba_ref, bi_ref, ap_ref, o_ref, conv_ref, h_ref):
    u = u_ref[...]
    xc = cb_ref[...]
    for j in range(CONV_W - 1):
        xc = xc + sc_ref[j] * cw_ref[j:j + 1, :]
    xc = xc + u * cw_ref[CONV_W - 1:CONV_W, :]
    a, b = _lru_gates(xc, wa_ref, wi_ref, ba_ref, bi_ref, ap_ref)
    h = a * h0_ref[...] + b
    h_ref[...] = h
    for j in range(CONV_W - 2):
        conv_ref[j] = sc_ref[j + 1]
    conv_ref[CONV_W - 2] = u
    o_ref[...] = (h * jax.nn.gelu(gate_ref[...])).astype(bf16)


def _lru_decode_call(l, u, gate, sc_t, h0, cw, cb, wa, wi, ba, bi, ap):
    rows = pl.BlockSpec((DEC_BATCH, LRU_W), lambda i: (N_PROMPT // DEC_BATCH, 0))
    vec = lambda: pl.BlockSpec((None, 1, LRU_W), lambda i: (l, 0, 0))
    return pl.pallas_call(
        _lru_decode_kernel,
        grid=(1,),
        in_specs=[rows, rows,
                  pl.BlockSpec((None, CONV_W - 1, DEC_BATCH, LRU_W), lambda i: (l, 0, 0, 0)),
                  pl.BlockSpec((None, DEC_BATCH, LRU_W), lambda i: (l, 0, 0)),
                  pl.BlockSpec((None, CONV_W, LRU_W), lambda i: (l, 0, 0)), vec(),
                  pl.BlockSpec((None, 2, LRU_W // 2, LRU_W // 2), lambda i: (l, 0, 0, 0)),
                  pl.BlockSpec((None, 2, LRU_W // 2, LRU_W // 2), lambda i: (l, 0, 0, 0)),
                  vec(), vec(), vec()],
        out_specs=[pl.BlockSpec((DEC_BATCH, LRU_W), lambda i: (0, 0)),
                   pl.BlockSpec((CONV_W - 1, DEC_BATCH, LRU_W), lambda i: (0, 0, 0)),
                   pl.BlockSpec((DEC_BATCH, LRU_W), lambda i: (0, 0))],
        out_shape=[jax.ShapeDtypeStruct((DEC_BATCH, LRU_W), bf16),
                   jax.ShapeDtypeStruct((CONV_W - 1, DEC_BATCH, LRU_W), f32),
                   jax.ShapeDtypeStruct((DEC_BATCH, LRU_W), f32)],
        name=f"rglru_decode_{l}",
    )(u, gate, sc_t, h0, cw, cb, wa, wi, ba, bi, ap)


def _subln(o, gs, lam_init):
    ms = jnp.mean(o * o, axis=-1, keepdims=True)
    return (o * lax.rsqrt(ms + EPS) * gs) * (1.0 - lam_init)


def _attn_prompt_kernel(qt_ref, k_ref, vt_ref, wl_ref, gsb_ref, o_ref,
                        s_buf0, s_buf1, m1, a1, m2, a2, *, lam_init):
    qi = pl.program_id(2)
    q = qt_ref[...].astype(f32)
    comp_row = lax.broadcasted_iota(jnp.int32, (V_HD, TQ), 0)
    qs = (jnp.where(comp_row < QK_HD, q, 0.0).astype(bf16),
          jnp.where(comp_row >= QK_HD, q, 0.0).astype(bf16))
    stats = ((m1, a1), (m2, a2))
    for m_ref, a_ref in stats:
        m_ref[...] = jnp.full((1, TQ), NEG, f32)
        a_ref[...] = jnp.zeros((V_HD + SUB, TQ), f32)
    ones = jnp.ones((SUB, TQ), bf16)

    def scores(off, s_ref, masked):
        kb = k_ref[pl.ds(pl.multiple_of(off, TQ), TQ), :]
        for c in range(2):
            s = jnp.dot(kb, qs[c], preferred_element_type=f32)
            if masked:
                key = lax.broadcasted_iota(jnp.int32, (TQ, TQ), 0)
                qry = lax.broadcasted_iota(jnp.int32, (TQ, TQ), 1)
                s = jnp.where(key <= qry, s, NEG)
            s_ref[c] = s

    def consume(off, s_ref):
        vt = jnp.concatenate([vt_ref[:, pl.ds(pl.multiple_of(off, TQ), TQ)], ones], axis=0)
        for c, (m_ref, a_ref) in enumerate(stats):
            s = s_ref[c]
            m_old = m_ref[...]
            m_new = jnp.maximum(m_old, jnp.max(s, axis=0, keepdims=True))
            alpha = jnp.exp2(m_old - m_new)
            p = jnp.exp2(s - m_new).astype(bf16)
            a_ref[...] = alpha * a_ref[...] + jnp.dot(vt, p, preferred_element_type=f32)
            m_ref[...] = m_new

    diag = qi * TQ
    scores(diag, s_buf0, True)

    def pair(u, prev_off):
        off1 = 2 * u * TQ
        scores(off1, s_buf1, False)
        consume(prev_off, s_buf0)
        off2 = off1 + TQ
        scores(off2, s_buf0, False)
        consume(off1, s_buf1)
        return off2

    last_off = lax.fori_loop(0, qi // 2, pair, diag)

    @pl.when(qi % 2 == 1)
    def _():
        off = (qi - 1) * TQ
        scores(off, s_buf1, False)
        consume(last_off, s_buf0)
        consume(off, s_buf1)

    @pl.when(qi % 2 == 0)
    def _():
        consume(last_off, s_buf0)

    lam = _lam(wl_ref, lam_init)
    inv1 = 1.0 / a1[V_HD:V_HD + 1, :]
    inv2 = 1.0 / a2[V_HD:V_HD + 1, :]
    o = a1[:V_HD, :] * inv1 - lam * (a2[:V_HD, :] * inv2)
    ms = jnp.mean(o * o, axis=0, keepdims=True)
    y = (o * lax.rsqrt(ms + EPS) * gsb_ref[...]) * (1.0 - lam_init)
    o_ref[...] = y.astype(bf16)


def _attn_prompt_call(l, qt, kb, vt, w_lambda, gsb):
    lam_init = 0.8 - 0.6 * math.exp(-0.3 * l)
    nq = SEQ // TQ
    qo_spec = pl.BlockSpec((V_HD, TQ), lambda b, h, i: (h, b * nq + i))
    return pl.pallas_call(
        functools.partial(_attn_prompt_kernel, lam_init=lam_init),
        grid=(BATCH, N_HEADS, nq),
        in_specs=[qo_spec,
                  pl.BlockSpec((SEQ, V_HD), lambda b, h, i: (b, h)),
                  pl.BlockSpec((V_HD, SEQ), lambda b, h, i: (h, b)),
                  pl.BlockSpec((None, 4, QK_HD), lambda b, h, i: (l, 0, 0)),
                  pl.BlockSpec((None, V_HD, TQ), lambda b, h, i: (l, 0, 0))],
        out_specs=qo_spec,
        out_shape=jax.ShapeDtypeStruct((ATT_W, N_PROMPT), bf16),
        scratch_shapes=[pltpu.VMEM((2, TQ, TQ), f32), pltpu.VMEM((2, TQ, TQ), f32),
                        pltpu.VMEM((1, TQ), f32), pltpu.VMEM((V_HD + SUB, TQ), f32),
                        pltpu.VMEM((1, TQ), f32), pltpu.VMEM((V_HD + SUB, TQ), f32)],
        compiler_params=pltpu.CompilerParams(
            dimension_semantics=("arbitrary", "arbitrary", "arbitrary"),
            vmem_limit_bytes=VMEM_LIMIT),
        name=f"attn_prompt_{l}",
    )(qt, kb, vt, w_lambda, gsb)


def _attn_decode_kernel(pt_ref, q_ref, kn_ref, vn_ref, wl_ref, gs_ref, *rest, lam_init):
    del pt_ref
    k_refs = rest[:N_PAGES]
    v_refs = rest[N_PAGES:2 * N_PAGES]
    o_ref, e_s, enew_s, coef_s = rest[2 * N_PAGES:]

    @pl.when(pl.program_id(0) == 0)
    def _():
        e_s[...] = jnp.zeros_like(e_s)
        enew_s[...] = jnp.zeros_like(enew_s)
        coef_s[...] = jnp.zeros_like(coef_s)

    o = _decode_values(e_s[...], enew_s[...], coef_s[...], vn_ref[...], v_refs)
    o_ref[...] = _subln(o, gs_ref[...], lam_init)
    e, e_new, coef = _decode_scores(q_ref[...].astype(f32), kn_ref[...], k_refs, _lam(wl_ref, lam_init))
    e_s[...] = e
    enew_s[...] = e_new
    coef_s[...] = coef


_NT = (((1,), (1,)), ((), ()))
_NC = 2 * N_HEADS


def _decode_scores(q, kn, k_refs, lam):
    nc = _NC
    nt = _NT
    row8 = lax.broadcasted_iota(jnp.int32, (nc, V_HD), 0)
    lane8 = lax.broadcasted_iota(jnp.int32, (nc, V_HD), 1)
    lane1 = lax.broadcasted_iota(jnp.int32, (1, V_HD), 1)
    q8 = jnp.where(jnp.right_shift(lane8, 6) == jnp.bitwise_and(row8, 1), q, 0.0)
    q8 = q8 * QK_SCALE
    q_rows = jnp.concatenate([q8] * (V_HD // nc), axis=0)
    q_rows = jnp.concatenate([q_rows] * N_PAGES, axis=1)
    rq = lax.broadcasted_iota(jnp.int32, (V_HD, N_PAGES * V_HD), 0)
    lq = lax.broadcasted_iota(jnp.int32, (V_HD, N_PAGES * V_HD), 1)
    qmat = jnp.where(jnp.right_shift(rq, 3) == jnp.right_shift(lq, 7), q_rows, 0.0).astype(bf16)

    half = N_PAGES // 2
    s = None
    for g in range(2):
        kcat = jnp.concatenate([k_refs[p][...].astype(bf16) for p in range(g * half, (g + 1) * half)],
                               axis=1)
        sg = lax.dot_general(kcat, qmat[:, g * half * V_HD:(g + 1) * half * V_HD], nt,
                             preferred_element_type=f32)
        s = sg if s is None else s + sg
    rowp = lax.broadcasted_iota(jnp.int32, (PAGE_ROWS, V_HD), 0)
    colp = lax.broadcasted_iota(jnp.int32, (PAGE_ROWS, V_HD), 1)
    own = jnp.bitwise_and(rowp, N_HEADS - 1) == jnp.right_shift(jnp.bitwise_and(colp, nc - 1), 1)
    s = jnp.where(own, s, NEG)
    s_new = lax.dot_general(kn.astype(bf16), qmat[:, :V_HD], nt, preferred_element_type=f32)
    own8 = (jnp.where(row8 < N_HEADS, row8, -1)
            == jnp.where(lane8 < nc, jnp.right_shift(lane8, 1), -2))
    s_new = jnp.where(own8, s_new, NEG)

    def over_pages(x, op):
        x = jnp.broadcast_to(x, (SUB, V_HD))
        for sh in (nc, 2 * nc, 4 * nc, 8 * nc):
            x = op(x, pltpu.roll(x, sh, 1))
        return x[0:1, :]

    m = over_pages(jnp.maximum(jnp.max(s, axis=0, keepdims=True),
                               jnp.max(s_new, axis=0, keepdims=True)), jnp.maximum)
    e = jnp.exp(s - m)
    e_new = jnp.exp(s_new - m)
    denom = over_pages(jnp.sum(e, axis=0, keepdims=True) + jnp.sum(e_new, axis=0, keepdims=True),
                       jnp.add)
    coef = jnp.where(jnp.bitwise_and(lane1, 1) == 0, 1.0, -lam) / denom
    return e.astype(bf16), e_new.astype(bf16), coef


def _decode_values(e, e_new, coef, vn, v_refs):
    nc = _NC
    lane1 = lax.broadcasted_iota(jnp.int32, (1, V_HD), 1)

    def lane_weights(first_lane):
        sel = jnp.where(jnp.right_shift(lane1, 3) == first_lane // nc, coef, 0.0)
        return jnp.broadcast_to(sel, (V_HD, V_HD)).astype(bf16)

    w_new = lax.dot_general(e_new, lane_weights(0), _NT, preferred_element_type=f32)
    acc = w_new * vn
    for p in range(0, N_PAGES, 2):
        both = jnp.concatenate([lane_weights(nc * p), lane_weights(nc * (p + 1))], axis=0)
        w = lax.dot_general(e, both, _NT, preferred_element_type=f32)
        wv = w[:, :V_HD] * v_refs[p][...] + w[:, V_HD:] * v_refs[p + 1][...]
        acc = acc + jnp.sum(wv.reshape(PAGE_ROWS // SUB, SUB, V_HD), axis=0)
    return acc[:N_HEADS, :] + acc[N_HEADS:, :]


def _attn_decode_call(l, page_table, q_rep, kn8, vn8, cache_k, cache_v, w_lambda, g_subln3):
    lam_init = 0.8 - 0.6 * math.exp(-0.3 * l)
    last = DEC_BATCH - 1
    cur = lambda t: jnp.minimum(t, last)
    prev = lambda t: jnp.maximum(t - 1, 0)
    row_spec = lambda tok: pl.BlockSpec((None, _NC, V_HD), lambda t, pt: (tok(t), 0, 0))

    def page_spec(tok, p):
        return pl.BlockSpec((None, None, PAGE_ROWS, V_HD), lambda t, pt: (l, pt[tok(t), p], 0, 0))

    grid_spec = pltpu.PrefetchScalarGridSpec(
        num_scalar_prefetch=1,
        grid=(DEC_BATCH + 1,),
        in_specs=[row_spec(cur), row_spec(cur), row_spec(prev),
                  pl.BlockSpec((None, 4, QK_HD), lambda t, pt: (l, 0, 0)),
                  pl.BlockSpec((None, 1, V_HD), lambda t, pt: (l, 0, 0))]
                 + [page_spec(cur, p) for p in range(N_PAGES)]
                 + [page_spec(prev, p) for p in range(N_PAGES)],
        out_specs=pl.BlockSpec((None, N_HEADS, V_HD), lambda t, pt: (prev(t), 0, 0)),
        scratch_shapes=[pltpu.VMEM((PAGE_ROWS, V_HD), bf16), pltpu.VMEM((_NC, V_HD), bf16),
                        pltpu.VMEM((1, V_HD), f32)],
    )
    return pl.pallas_call(
        functools.partial(_attn_decode_kernel, lam_init=lam_init),
        grid_spec=grid_spec,
        out_shape=jax.ShapeDtypeStruct((DEC_BATCH, N_HEADS, V_HD), f32),
        compiler_params=pltpu.CompilerParams(dimension_semantics=("arbitrary",),
                                             vmem_limit_bytes=VMEM_LIMIT),
        name=f"attn_decode_{l}",
    )(page_table, q_rep, kn8, vn8, w_lambda, g_subln3,
      *([cache_k] * N_PAGES), *([cache_v] * N_PAGES))


def _block_diag(w):
    per = LRU_BLOCKS // 2
    w = w.reshape(DEPTH, 2, per, LRU_BS, LRU_BS)
    eye = jnp.eye(per, dtype=w.dtype)
    return jnp.einsum('dgnij,nm->dgnimj', w, eye).reshape(DEPTH, 2, per * LRU_BS, per * LRU_BS)


@jax.jit
def kernel(x_prompt, x_sample, cache_k, cache_v, page_table, state_conv, state_h, p_prompt, p_sample,
           w_in, w_out, conv_w, conv_b, w_a, b_a, w_i, b_i, a_param, w_lambda, g_subln,
           w_ffn_gate, w_ffn_up, w_ffn_down, w_pe, w_pg, g_norm, g_final):
    n_dec = DEC_BATCH
    x = x_prompt.reshape(N_PROMPT, D_MODEL)
    x_dec = x_sample.reshape(n_dec, D_MODEL)
    p_pr = p_prompt.reshape(DEPTH, N_PROMPT, P_DIM)
    p_dec = p_sample.reshape(DEPTH, n_dec, P_DIM)

    wg_b = w_ffn_gate.astype(bf16)
    wu_b = w_ffn_up.astype(bf16)
    wd_b = w_ffn_down.astype(bf16)
    win_b = w_in.astype(bf16)
    wo_b = w_out.astype(bf16)
    wpg_b = w_pg.astype(bf16)
    wpe_b = w_pe.astype(bf16)
    wa_b = _block_diag(w_a).astype(bf16)
    wi_b = _block_diag(w_i).astype(bf16)
    g_norm3 = g_norm.reshape(DEPTH, 4, 1, D_MODEL)
    g_final2 = g_final.reshape(1, D_MODEL)
    g_subln3 = g_subln.reshape(DEPTH, 1, V_HD)
    cb3 = conv_b.reshape(DEPTH, 1, LRU_W)
    ba3 = b_a.reshape(DEPTH, 1, LRU_W)
    bi3 = b_i.reshape(DEPTH, 1, LRU_W)
    ap3 = a_param.reshape(DEPTH, 1, LRU_W)
    sc_t = state_conv.transpose(0, 2, 1, 3)
    ck = cache_k.reshape(DEPTH, -1, PAGE_ROWS, V_HD)
    cv = cache_v.reshape(DEPTH, -1, PAGE_ROWS, V_HD)
    gsb = jnp.broadcast_to(g_subln[:, :, None], (DEPTH, V_HD, TQ))
    tabs = _rope_tables()

    ks_s, vs_s, convs_p, hs_p, convs_s, hs_s = [], [], [], [], [], []
    kfin = vfin = None
    dec_rows = n_dec * N_HEADS
    for l in range(DEPTH):
        x, u, gate, qs, qt, kb, vt, kfin, vfin, ksn, vsn = _x_call(
            l, x, x_dec, g_norm3, wg_b, wu_b, wd_b, win_b, tabs, kfin, vfin)
        k_new = ksn[:dec_rows].reshape(n_dec, N_HEADS, V_HD)
        v_new = vsn[:dec_rows].reshape(n_dec, N_HEADS, V_HD)
        ks_s.append(k_new)
        vs_s.append(v_new)

        o_lru, conv_p, h_p = _lru_prompt_call(l, u, gate, conv_w, cb3, wa_b, wi_b, ba3, bi3, ap3)
        o_lru_s, conv_s, h_s = _lru_decode_call(l, u, gate, sc_t, state_h,
                                                conv_w, cb3, wa_b, wi_b, ba3, bi3, ap3)
        convs_p.append(conv_p)
        hs_p.append(h_p.reshape(BATCH, LRU_W))
        convs_s.append(conv_s.transpose(1, 0, 2))
        hs_s.append(h_s)

        o_attn_t = _attn_prompt_call(l, qt, kb, vt, w_lambda, gsb)
        q_rep = jnp.repeat(qs[:n_dec].reshape(n_dec, N_HEADS, V_HD), 2, axis=1)
        pad8 = ((0, 0), (0, 2 * N_HEADS - N_HEADS), (0, 0))
        o_attn_s = _attn_decode_call(l, page_table, q_rep, jnp.pad(k_new, pad8), jnp.pad(v_new, pad8),
                                     ck, cv, w_lambda, g_subln3)
        o_attn_t_s = o_attn_s.reshape(n_dec, ATT_W).astype(bf16).T

        x = _y_call(l, x, o_lru, o_lru_s, o_attn_t, o_attn_t_s, p_pr, p_dec, wo_b, g_norm3,
                    wg_b, wu_b, wd_b, wpg_b, wpe_b, g_final2)
    y_p, y_s = x
    y_prompt = y_p.reshape(BATCH, SEQ, D_MODEL)
    y_sample = y_s[:n_dec].reshape(n_dec, 1, D_MODEL)
    k_prompt = kfin.reshape(DEPTH, BATCH, SEQ, N_HEADS, V_HD)
    v_prompt = vfin.reshape(DEPTH, BATCH, SEQ, N_HEADS, V_HD)
    k_sample = jnp.stack(ks_s).reshape(DEPTH, n_dec, 1, N_HEADS, V_HD)
    v_sample = jnp.stack(vs_s).reshape(DEPTH, n_dec, 1, N_HEADS, V_HD)
    return (y_prompt, y_sample, k_prompt, v_prompt, jnp.stack(convs_p), jnp.stack(hs_p),
            k_sample, v_sample, jnp.stack(convs_s), jnp.stack(hs_s))
```

```python
import functools
import math

import jax
import jax.numpy as jnp
from jax import lax
from jax.experimental import pallas as pl
from jax.experimental.pallas import tpu as pltpu

f32 = jnp.float32
bf16 = jnp.bfloat16

D_MODEL = 1024
BATCH = 4
SEQ = 4096
DEPTH = 4
DEC_BATCH = 128
PAST_LEN = 2048
PAGE_SIZE = 128
N_PAGES = PAST_LEN // PAGE_SIZE
N_HEADS = 4
PAGE_ROWS = PAGE_SIZE * N_HEADS
LRU_W = 512
LRU_BLOCKS = 8
LRU_BS = 64
CONV_W = 4
C_GATE = 8.0
ATT_W = 512
QK_HD = 64
V_HD = 128
ROT_DIM = 16
ROPE_THETA = 500000.0
D_FF = 2816
P_DIM = 256
EPS = 1e-6
NEG = -1e30
QK_SCALE = QK_HD ** -0.5
LOG2E = math.log2(math.e)

N_PROMPT = BATCH * SEQ
TM = 512
N_TILES_P = N_PROMPT // TM
N_TILES = N_TILES_P + 1
N_PAD = N_TILES * TM
SEQ_TILES = SEQ // TM
MXU_COLS = 256
FF_SPLIT = (D_FF // MXU_COLS + 1) // 2 * MXU_COLS
FF_CHUNKS = ((0, FF_SPLIT), (FF_SPLIT, D_FF))
TQ = 512
TT = 512
SUB = 8
VMEM_LIMIT = 56 * 1024 * 1024


def _const_spec(shape):
    nd = len(shape)
    return pl.BlockSpec(shape, lambda *_: (0,) * nd, pipeline_mode=pl.Buffered(1))


def _layer_spec(shape, lead):
    nl = len(lead)
    nd = len(shape)
    return pl.BlockSpec((None,) * nl + tuple(shape), lambda *_: tuple(lead) + (0,) * nd,
                        pipeline_mode=pl.Buffered(1))


def _rms(x, g):
    ms = jnp.mean(x * x, axis=-1, keepdims=True)
    return x * lax.rsqrt(ms + EPS) * g


def _swiglu(hb, wg_ref, wu_ref, wd_ref):
    acc = None
    for lo, hi in FF_CHUNKS:
        sl = slice(lo, hi)
        g = jnp.dot(hb, wg_ref[:, sl], preferred_element_type=f32)
        u = jnp.dot(hb, wu_ref[:, sl], preferred_element_type=f32)
        a = (g * jax.nn.sigmoid(g) * u).astype(bf16)
        y = jnp.dot(a, wd_ref[sl, :], preferred_element_type=f32)
        acc = y if acc is None else acc + y
    return acc


def _lam(wl_ref, lam_init):
    wl = wl_ref[...]
    s01 = jnp.sum(wl[0:1, :] * wl[1:2, :], axis=-1, keepdims=True)
    s23 = jnp.sum(wl[2:3, :] * wl[3:4, :], axis=-1, keepdims=True)
    return jnp.exp(s01) - jnp.exp(s23) + lam_init


def _rope_table_kernel(freq_ref, c_ref, sa_ref, sb_ref):
    rows = c_ref.shape[0]
    row = lax.broadcasted_iota(jnp.int32, (rows, V_HD), 0)
    lane = lax.broadcasted_iota(jnp.int32, (rows, V_HD), 1)
    pos = jnp.where(row < SEQ, row, PAST_LEN).astype(f32)
    ang = pos * freq_ref[...]
    cos = jnp.cos(ang)
    sin = jnp.sin(ang)
    in_comp = lane % QK_HD
    half = ROT_DIM // 2
    c_ref[...] = jnp.where(in_comp < ROT_DIM, cos, 1.0)
    sa_ref[...] = jnp.where(in_comp < half, -sin, 0.0)
    sb_ref[...] = jnp.where((in_comp >= half) & (in_comp < ROT_DIM), sin, 0.0)


def _rope_tables():
    half = ROT_DIM // 2
    freqs = jnp.power(jnp.float32(ROPE_THETA), -jnp.arange(0, ROT_DIM, 2, dtype=f32) / ROT_DIM)
    lane = jnp.arange(V_HD)
    freq_lane = freqs[(lane % QK_HD) % half].reshape(1, V_HD)
    rows = SEQ + TM
    shp = jax.ShapeDtypeStruct((rows, V_HD), f32)
    return pl.pallas_call(
        _rope_table_kernel,
        out_shape=(shp, shp, shp),
        name="rope_tables",
    )(freq_lane)


def _store_heads_interleaved(dst_ref, t):
    for hd in range(N_HEADS):
        dst_ref[pl.ds(hd, TM, stride=N_HEADS), :] = t[:, hd * V_HD:(hd + 1) * V_HD]


def _token_tile(i, prompt_ref, dec_ref, axis=0):
    dec = dec_ref[...]
    pad_shape = list(dec.shape)
    pad_shape[axis] = TM - dec.shape[axis]
    dec_tile = jnp.concatenate([dec, jnp.zeros(pad_shape, dec.dtype)], axis=axis)
    return jnp.where(i < N_TILES_P, prompt_ref[...], dec_tile)


def _x_kernel(x_ref, xdec_ref, gn0_ref, gn1_ref, wg_ref, wu_ref, wd_ref, win_ref, c_ref, sa_ref, sb_ref,
              *rest, split_input):
    (xo_ref, u_ref, gate_ref, qs_ref, qt_ref, kb_ref, vt_ref,
     kfin_ref, vfin_ref, ksn_ref, vsn_ref) = rest[-11:]
    i = pl.program_id(0)
    x = _token_tile(i, x_ref, xdec_ref) if split_input else x_ref[...]
    h = _rms(x, gn0_ref[...]).astype(bf16)
    x1 = x + 0.5 * _swiglu(h, wg_ref, wu_ref, wd_ref)
    xo_ref[...] = x1
    h1 = _rms(x1, gn1_ref[...]).astype(bf16)

    def proj(i):
        return jnp.dot(h1, win_ref[:, i * LRU_W:(i + 1) * LRU_W], preferred_element_type=f32)

    u_ref[...] = proj(0)
    gate_ref[...] = proj(1)
    cos = c_ref[...]
    sa = sa_ref[...]
    sb = sb_ref[...]

    def rope(t):
        outs = []
        for hd in range(N_HEADS):
            th = t[:, hd * V_HD:(hd + 1) * V_HD]
            up = pltpu.roll(th, V_HD - ROT_DIM // 2, 1)
            dn = pltpu.roll(th, ROT_DIM // 2, 1)
            outs.append(th * cos + up * sa + dn * sb)
        return jnp.concatenate(outs, axis=-1)

    q = rope(proj(2))
    qt_ref[...] = (q * (QK_SCALE * LOG2E)).T.astype(bf16)
    k = rope(proj(3))
    kb_ref[...] = k.astype(bf16)
    v = proj(4)
    vt_ref[...] = v.T.astype(bf16)

    @pl.when(i < N_TILES_P)
    def _():
        _store_heads_interleaved(kfin_ref, k)
        _store_heads_interleaved(vfin_ref, v)

    @pl.when(i == N_TILES_P)
    def _():
        qs_ref[...] = q.astype(bf16)
        _store_heads_interleaved(ksn_ref, k)
        _store_heads_interleaved(vsn_ref, v)


def _x_call(l, x, x_dec, g_norm3, wg, wu, wd, win, tabs, kfin, vfin):
    split_input = x.shape[0] == N_PROMPT
    tile = lambda w: pl.BlockSpec((TM, w), lambda i: (i, 0))
    x_spec = (pl.BlockSpec((TM, D_MODEL), lambda i: (jnp.minimum(i, N_TILES_P - 1), 0))
              if split_input else tile(D_MODEL))
    tile_t = pl.BlockSpec((LRU_W, TM), lambda i: (0, i))
    tab_spec = pl.BlockSpec((TM, V_HD), lambda i: (jnp.where(i < N_TILES_P, i % SEQ_TILES, SEQ_TILES), 0))
    fin_spec = pl.BlockSpec((None, TM * N_HEADS, V_HD), lambda i: (l, jnp.minimum(i, N_TILES_P - 1), 0))
    dec_spec = lambda w: pl.BlockSpec((TM * N_HEADS, w), lambda i: (0, 0))
    act = lambda dt: jax.ShapeDtypeStruct((N_PAD, LRU_W), dt)
    act_t = jax.ShapeDtypeStruct((LRU_W, N_PAD), bf16)
    fin = jax.ShapeDtypeStruct((DEPTH, N_PROMPT * N_HEADS, V_HD), f32)
    dec = jax.ShapeDtypeStruct((TM * N_HEADS, V_HD), f32)
    in_specs = [x_spec, _const_spec(x_dec.shape),
                _layer_spec((1, D_MODEL), (l, 0)), _layer_spec((1, D_MODEL), (l, 1)),
                _layer_spec((D_MODEL, D_FF), (l, 0)), _layer_spec((D_MODEL, D_FF), (l, 0)),
                _layer_spec((D_FF, D_MODEL), (l, 0)), _layer_spec((D_MODEL, 5 * LRU_W), (l,)),
                tab_spec, tab_spec, tab_spec]
    args = [x, x_dec, g_norm3, g_norm3, wg, wu, wd, win, *tabs]
    aliases = {}
    if kfin is not None:
        aliases = {len(args): 7, len(args) + 1: 8}
        in_specs += [pl.BlockSpec(memory_space=pl.ANY)] * 2
        args += [kfin, vfin]
    return pl.pallas_call(
        functools.partial(_x_kernel, split_input=split_input),
        grid=(N_TILES,),
        in_specs=in_specs,
        out_specs=[tile(D_MODEL), tile(LRU_W), tile(LRU_W),
                   pl.BlockSpec((TM, ATT_W), lambda i: (0, 0)), tile_t, tile(ATT_W), tile_t,
                   fin_spec, fin_spec, dec_spec(V_HD), dec_spec(V_HD)],
        out_shape=[jax.ShapeDtypeStruct((N_PAD, D_MODEL), f32), act(f32), act(f32),
                   jax.ShapeDtypeStruct((TM, ATT_W), bf16), act_t, act(bf16), act_t,
                   fin, fin, dec, dec],
        input_output_aliases=aliases,
        compiler_params=pltpu.CompilerParams(dimension_semantics=("arbitrary",),
                                             vmem_limit_bytes=VMEM_LIMIT),
        name=f"ffn1_inproj_{l}",
    )(*args)


def _y_kernel(x_ref, ol_ref, oldec_ref, oa_ref, oadec_ref, p_ref, pdec_ref, wo_ref, gn2_ref,
              wg_ref, wu_ref, wd_ref, gn3_ref, wpg_ref, wpe_ref, *rest, final):
    i = pl.program_id(0)
    x = x_ref[...]
    o_lru = _token_tile(i, ol_ref, oldec_ref)
    o_attn_t = _token_tile(i, oa_ref, oadec_ref, axis=1)
    x2 = (x + jnp.dot(o_lru, wo_ref[:LRU_W, :], preferred_element_type=f32)
          + lax.dot_general(o_attn_t, wo_ref[LRU_W:, :], (((0,), (0,)), ((), ())),
                            preferred_element_type=f32))
    h = _rms(x2, gn2_ref[...]).astype(bf16)
    x3 = x2 + 0.5 * _swiglu(h, wg_ref, wu_ref, wd_ref)
    h3 = _rms(x3, gn3_ref[...]).astype(bf16)
    gate = jax.nn.sigmoid(jnp.dot(h3, wpg_ref[...], preferred_element_type=f32))
    p = _token_tile(i, p_ref, pdec_ref)
    pe = jnp.dot(p.astype(bf16), wpe_ref[...], preferred_element_type=f32)
    x4 = x3 + gate * pe
    if final:
        gf_ref, yp_ref, ys_ref = rest
        y = _rms(x4, gf_ref[...])

        @pl.when(i < N_TILES_P)
        def _():
            yp_ref[...] = y

        @pl.when(i == N_TILES_P)
        def _():
            ys_ref[...] = y
    else:
        (xo_ref,) = rest
        xo_ref[...] = x4


def _y_call(l, x, o_lru, o_lru_dec, o_attn_t, o_attn_t_dec, p_prompt, p_dec, wo, g_norm3,
            wg, wu, wd, wpg, wpe, g_final):
    final = l == DEPTH - 1
    tile = lambda w: pl.BlockSpec((TM, w), lambda i: (i, 0))
    prompt_tile = lambda i: jnp.minimum(i, N_TILES_P - 1)
    in_specs = [tile(D_MODEL),
                pl.BlockSpec((TM, LRU_W), lambda i: (prompt_tile(i), 0)),
                pl.BlockSpec((DEC_BATCH, LRU_W), lambda i: (0, 0)),
                pl.BlockSpec((ATT_W, TM), lambda i: (0, prompt_tile(i))),
                pl.BlockSpec((ATT_W, DEC_BATCH), lambda i: (0, 0)),
                pl.BlockSpec((None, TM, P_DIM), lambda i: (l, prompt_tile(i), 0)),
                pl.BlockSpec((None, DEC_BATCH, P_DIM), lambda i: (l, 0, 0)),
                _layer_spec((D_MODEL, D_MODEL), (l,)),
                _layer_spec((1, D_MODEL), (l, 2)),
                _layer_spec((D_MODEL, D_FF), (l, 1)), _layer_spec((D_MODEL, D_FF), (l, 1)),
                _layer_spec((D_FF, D_MODEL), (l, 1)),
                _layer_spec((1, D_MODEL), (l, 3)),
                _layer_spec((D_MODEL, D_MODEL), (l,)), _layer_spec((P_DIM, D_MODEL), (l,))]
    args = [x, o_lru, o_lru_dec, o_attn_t, o_attn_t_dec, p_prompt, p_dec,
            wo, g_norm3, wg, wu, wd, g_norm3, wpg, wpe]
    out_specs = tile(D_MODEL)
    out_shape = jax.ShapeDtypeStruct((N_PAD, D_MODEL), f32)
    if final:
        in_specs.append(_const_spec((1, D_MODEL)))
        args.append(g_final)
        out_specs = [pl.BlockSpec((TM, D_MODEL), lambda i: (jnp.minimum(i, N_TILES_P - 1), 0)),
                     pl.BlockSpec((TM, D_MODEL), lambda i: (0, 0))]
        out_shape = [jax.ShapeDtypeStruct((N_PROMPT, D_MODEL), f32),
                     jax.ShapeDtypeStruct((TM, D_MODEL), f32)]
    return pl.pallas_call(
        functools.partial(_y_kernel, final=final),
        grid=(N_TILES,),
        in_specs=in_specs,
        out_specs=out_specs,
        out_shape=out_shape,
        compiler_params=pltpu.CompilerParams(dimension_semantics=("arbitrary",),
                                             vmem_limit_bytes=VMEM_LIMIT),
        name=f"outproj_ffn2_embed_{l}",
    )(*args)


def _softplus(x):
    return jnp.maximum(x, 0.0) + jnp.log1p(jnp.exp(-jnp.abs(x)))


def _lru_gates(xc, wa_ref, wi_ref, ba_ref, bi_ref, ap_ref):
    xb = xc.astype(bf16)
    half = LRU_W // 2

    def bd(w_ref):
        lo = jnp.dot(xb[:, :half], w_ref[0], preferred_element_type=f32)
        hi = jnp.dot(xb[:, half:], w_ref[1], preferred_element_type=f32)
        return jnp.concatenate([lo, hi], axis=-1)

    r = jax.nn.sigmoid(bd(wa_ref) + ba_ref[...])
    i = jax.nn.sigmoid(bd(wi_ref) + bi_ref[...])
    log_a = -C_GATE * r * _softplus(-ap_ref[...])
    a = jnp.exp(log_a)
    b = jnp.sqrt(-jnp.tanh(log_a) * (a * a + 1.0)) * (i * xc)
    return a, b


def _lru_prompt_kernel(u_ref, gate_ref, cw_ref, cb_ref, wa_ref, wi_ref, ba_ref, bi_ref, ap_ref,
                       o_ref, conv_ref, hlast_ref, ubuf, abuf, bbuf, hcar):
    t = pl.program_id(1)

    @pl.when(t == 0)
    def _():
        ubuf[0:SUB, :] = jnp.zeros((SUB, LRU_W), f32)
        hcar[...] = jnp.zeros((1, LRU_W), f32)

    ubuf[SUB:SUB + TT, :] = u_ref[...]
    xc = cb_ref[...]
    for j in range(CONV_W):
        off = SUB - (CONV_W - 1) + j
        xc = xc + ubuf[off:off + TT, :] * cw_ref[j:j + 1, :]
    conv_ref[...] = ubuf[TT + SUB - (CONV_W - 1):TT + SUB, :]
    ubuf[0:SUB, :] = ubuf[TT:TT + SUB, :]

    a, b = _lru_gates(xc, wa_ref, wi_ref, ba_ref, bi_ref, ap_ref)
    abuf[...] = a
    bbuf[...] = b

    row = lax.broadcasted_iota(jnp.int32, (SUB, LRU_W), 0)

    def group(g, h_prev):
        off = pl.multiple_of(g * SUB, SUB)
        ag = abuf[pl.ds(off, SUB), :]
        bg = bbuf[pl.ds(off, SUB), :]
        for d in (1, 2, 4):
            keep = row >= d
            a_sh = pltpu.roll(ag, d, 0)
            b_sh = pltpu.roll(bg, d, 0)
            bg = jnp.where(keep, ag * b_sh + bg, bg)
            ag = jnp.where(keep, ag * a_sh, ag)
        hg = ag * h_prev + bg
        bbuf[pl.ds(off, SUB), :] = hg
        return hg[SUB - 1:SUB, :]

    h_last = lax.fori_loop(0, TT // SUB, group, hcar[...])
    hcar[...] = h_last
    hlast_ref[...] = h_last
    o_ref[...] = (bbuf[...] * jax.nn.gelu(gate_ref[...])).astype(bf16)


def _lru_prompt_call(l, u, gate, cw, cb, wa, wi, ba, bi, ap):
    tile = pl.BlockSpec((TT, LRU_W), lambda b, t: (b * (SEQ // TT) + t, 0))
    vec = lambda: pl.BlockSpec((None, 1, LRU_W), lambda b, t: (l, 0, 0))
    return pl.pallas_call(
        _lru_prompt_kernel,
        grid=(BATCH, SEQ // TT),
        in_specs=[tile, tile,
                  pl.BlockSpec((None, CONV_W, LRU_W), lambda b, t: (l, 0, 0)), vec(),
                  pl.BlockSpec((None, 2, LRU_W // 2, LRU_W // 2), lambda b, t: (l, 0, 0, 0)),
                  pl.BlockSpec((None, 2, LRU_W // 2, LRU_W // 2), lambda b, t: (l, 0, 0, 0)),
                  vec(), vec(), vec()],
        out_specs=[tile,
                   pl.BlockSpec((None, CONV_W - 1, LRU_W), lambda b, t: (b, 0, 0)),
                   pl.BlockSpec((None, 1, LRU_W), lambda b, t: (b, 0, 0))],
        out_shape=[jax.ShapeDtypeStruct((N_PROMPT, LRU_W), bf16),
                   jax.ShapeDtypeStruct((BATCH, CONV_W - 1, LRU_W), f32),
                   jax.ShapeDtypeStruct((BATCH, 1, LRU_W), f32)],
        scratch_shapes=[pltpu.VMEM((TT + SUB, LRU_W), f32), pltpu.VMEM((TT, LRU_W), f32),
                        pltpu.VMEM((TT, LRU_W), f32), pltpu.VMEM((1, LRU_W), f32)],
        compiler_params=pltpu.CompilerParams(dimension_semantics=("arbitrary", "arbitrary"),
                                             vmem_limit_bytes=VMEM_LIMIT),
        name=f"rglru_prompt_{l}",
    )(u, gate, cw, cb, wa, wi, ba, bi, ap)


def _lru_decode_kernel(u_ref, gate_ref, sc_ref, h0_ref, cw_ref, cb_ref, wa_ref, wi_ref,
                       ba_ref, bi_ref, ap_ref, o_ref, conv_ref, h_ref):
    u = u_ref[...]
    xc = cb_ref[...]
    for j in range(CONV_W - 1):
        xc = xc + sc_ref[j] * cw_ref[j:j + 1, :]
    xc = xc + u * cw_ref[CONV_W - 1:CONV_W, :]
    a, b = _lru_gates(xc, wa_ref, wi_ref, ba_ref, bi_ref, ap_ref)
    h = a * h0_ref[...] + b
    h_ref[...] = h
    for j in range(CONV_W - 2):
        conv_ref[j] = sc_ref[j + 1]
    conv_ref[CONV_W - 2] = u
    o_ref[...] = (h * jax.nn.gelu(gate_ref[...])).astype(bf16)


def _lru_decode_call(l, u, gate, sc_t, h0, cw, cb, wa, wi, ba, bi, ap):
    rows = pl.BlockSpec((DEC_BATCH, LRU_W), lambda i: (N_PROMPT // DEC_BATCH, 0))
    vec = lambda: pl.BlockSpec((None, 1, LRU_W), lambda i: (l, 0, 0))
    return pl.pallas_call(
        _lru_decode_kernel,
        grid=(1,),
        in_specs=[rows, rows,
                  pl.BlockSpec((None, CONV_W - 1, DEC_BATCH, LRU_W), lambda i: (l, 0, 0, 0)),
                  pl.BlockSpec((None, DEC_BATCH, LRU_W), lambda i: (l, 0, 0)),
                  pl.BlockSpec((None, CONV_W, LRU_W), lambda i: (l, 0, 0)), vec(),
                  pl.BlockSpec((None, 2, LRU_W // 2, LRU_W // 2), lambda i: (l, 0, 0, 0)),
                  pl.BlockSpec((None, 2, LRU_W // 2, LRU_W // 2), lambda i: (l, 0, 0, 0)),
                  vec(), vec(), vec()],
        out_specs=[pl.BlockSpec((DEC_BATCH, LRU_W), lambda i: (0, 0)),
                   pl.BlockSpec((CONV_W - 1, DEC_BATCH, LRU_W), lambda i: (0, 0, 0)),
                   pl.BlockSpec((DEC_BATCH, LRU_W), lambda i: (0, 0))],
        out_shape=[jax.ShapeDtypeStruct((DEC_BATCH, LRU_W), bf16),
                   jax.ShapeDtypeStruct((CONV_W - 1, DEC_BATCH, LRU_W), f32),
                   jax.ShapeDtypeStruct((DEC_BATCH, LRU_W), f32)],
        name=f"rglru_decode_{l}",
    )(u, gate, sc_t, h0, cw, cb, wa, wi, ba, bi, ap)


def _subln(o, gs, lam_init):
    ms = jnp.mean(o * o, axis=-1, keepdims=True)
    return (o * lax.rsqrt(ms + EPS) * gs) * (1.0 - lam_init)


def _attn_prompt_kernel(qt_ref, k_ref, vt_ref, wl_ref, gsb_ref, o_ref,
                        s_buf0, s_buf1, *stat_refs, lam_init):
    j = pl.program_id(2)
    comp_row = lax.broadcasted_iota(jnp.int32, (V_HD, TQ), 0)
    ones = jnp.ones((SUB, TQ), bf16)
    lam = _lam(wl_ref, lam_init)

    for half in range(2):
        qi = 2 * j + half
        cols = slice(half * TQ, (half + 1) * TQ)
        q = qt_ref[:, cols].astype(f32)
        qs = (jnp.where(comp_row < QK_HD, q, 0.0).astype(bf16),
              jnp.where(comp_row >= QK_HD, q, 0.0).astype(bf16))
        stats = ((stat_refs[4 * half], stat_refs[4 * half + 1]),
                 (stat_refs[4 * half + 2], stat_refs[4 * half + 3]))
        buf_a, buf_b = (s_buf0, s_buf1) if half == 0 else (s_buf1, s_buf0)
        for m_ref, a_ref in stats:
            m_ref[...] = jnp.full((1, TQ), NEG, f32)
            a_ref[...] = jnp.zeros((V_HD + SUB, TQ), f32)

        def scores(off, s_ref, masked, qs=qs):
            kb = k_ref[pl.ds(pl.multiple_of(off, TQ), TQ), :]
            for c in range(2):
                s = jnp.dot(kb, qs[c], preferred_element_type=f32)
                if masked:
                    key = lax.broadcasted_iota(jnp.int32, (TQ, TQ), 0)
                    qry = lax.broadcasted_iota(jnp.int32, (TQ, TQ), 1)
                    s = jnp.where(key <= qry, s, NEG)
                s_ref[c] = s

        def consume(off, s_ref, stats=stats):
            vt = jnp.concatenate([vt_ref[:, pl.ds(pl.multiple_of(off, TQ), TQ)], ones], axis=0)
            for c, (m_ref, a_ref) in enumerate(stats):
                s = s_ref[c]
                m_old = m_ref[...]
                m_new = jnp.maximum(m_old, jnp.max(s, axis=0, keepdims=True))
                alpha = jnp.exp2(m_old - m_new)
                p = jnp.exp2(s - m_new).astype(bf16)
                a_ref[...] = alpha * a_ref[...] + jnp.dot(vt, p, preferred_element_type=f32)
                m_ref[...] = m_new

        diag = qi * TQ
        scores(diag, buf_a, True)

        def pair(u, prev_off, scores=scores, consume=consume, buf_a=buf_a, buf_b=buf_b):
            off1 = 2 * u * TQ
            scores(off1, buf_b, False)
            consume(prev_off, buf_a)
            off2 = off1 + TQ
            scores(off2, buf_a, False)
            consume(off1, buf_b)
            return off2

        last_off = lax.fori_loop(0, j, pair, diag)
        if half == 1:
            off = 2 * j * TQ
            scores(off, buf_b, False)
            consume(last_off, buf_a)
            consume(off, buf_b)
        else:
            consume(last_off, buf_a)

        (_, a1), (_, a2) = stats
        inv1 = 1.0 / a1[V_HD:V_HD + 1, :]
        inv2 = 1.0 / a2[V_HD:V_HD + 1, :]
        o = a1[:V_HD, :] * inv1 - lam * (a2[:V_HD, :] * inv2)
        ms = jnp.mean(o * o, axis=0, keepdims=True)
        y = (o * lax.rsqrt(ms + EPS) * gsb_ref[...]) * (1.0 - lam_init)
        o_ref[:, cols] = y.astype(bf16)


def _attn_prompt_call(l, qt, kb, vt, w_lambda, gsb):
    lam_init = 0.8 - 0.6 * math.exp(-0.3 * l)
    nq = SEQ // (2 * TQ)
    qo_spec = pl.BlockSpec((V_HD, 2 * TQ), lambda b, h, i: (h, b * nq + i))
    stat = [pltpu.VMEM((1, TQ), f32), pltpu.VMEM((V_HD + SUB, TQ), f32)]
    return pl.pallas_call(
        functools.partial(_attn_prompt_kernel, lam_init=lam_init),
        grid=(BATCH, N_HEADS, nq),
        in_specs=[qo_spec,
                  pl.BlockSpec((SEQ, V_HD), lambda b, h, i: (b, h)),
                  pl.BlockSpec((V_HD, SEQ), lambda b, h, i: (h, b)),
                  pl.BlockSpec((None, 4, QK_HD), lambda b, h, i: (l, 0, 0)),
                  pl.BlockSpec((None, V_HD, TQ), lambda b, h, i: (l, 0, 0))],
        out_specs=qo_spec,
        out_shape=jax.ShapeDtypeStruct((ATT_W, N_PROMPT), bf16),
        scratch_shapes=[pltpu.VMEM((2, TQ, TQ), f32), pltpu.VMEM((2, TQ, TQ), f32)] + stat * 4,
        compiler_params=pltpu.CompilerParams(
            dimension_semantics=("arbitrary", "arbitrary", "arbitrary"),
            vmem_limit_bytes=VMEM_LIMIT),
        name=f"attn_prompt_{l}",
    )(qt, kb, vt, w_lambda, gsb)


def _attn_decode_kernel(pt_ref, q_ref, kn_ref, vn_ref, wl_ref, gs_ref, *rest, lam_init):
    del pt_ref
    k_refs = rest[:N_PAGES]
    v_refs = rest[N_PAGES:2 * N_PAGES]
    o_ref, e_s, enew_s, coef_s = rest[2 * N_PAGES:]

    @pl.when(pl.program_id(0) == 0)
    def _():
        e_s[...] = jnp.zeros_like(e_s)
        enew_s[...] = jnp.zeros_like(enew_s)
        coef_s[...] = jnp.zeros_like(coef_s)

    o = _decode_values(e_s[...], enew_s[...], coef_s[...], vn_ref[...], v_refs)
    o_ref[...] = _subln(o, gs_ref[...], lam_init)
    e, e_new, coef = _decode_scores(q_ref[...].astype(f32), kn_ref[...], k_refs, _lam(wl_ref, lam_init))
    e_s[...] = e
    enew_s[...] = e_new
    coef_s[...] = coef


_NT = (((1,), (1,)), ((), ()))
_NC = 2 * N_HEADS


def _decode_scores(q, kn, k_refs, lam):
    nc = _NC
    nt = _NT
    row8 = lax.broadcasted_iota(jnp.int32, (nc, V_HD), 0)
    lane8 = lax.broadcasted_iota(jnp.int32, (nc, V_HD), 1)
    lane1 = lax.broadcasted_iota(jnp.int32, (1, V_HD), 1)
    q8 = jnp.where(jnp.right_shift(lane8, 6) == jnp.bitwise_and(row8, 1), q, 0.0)
    q8 = q8 * QK_SCALE
    q_rows = jnp.concatenate([q8] * (V_HD // nc), axis=0)
    q_rows = jnp.concatenate([q_rows] * N_PAGES, axis=1)
    rq = lax.broadcasted_iota(jnp.int32, (V_HD, N_PAGES * V_HD), 0)
    lq = lax.broadcasted_iota(jnp.int32, (V_HD, N_PAGES * V_HD), 1)
    qmat = jnp.where(jnp.right_shift(rq, 3) == jnp.right_shift(lq, 7), q_rows, 0.0).astype(bf16)

    half = N_PAGES // 2
    s = None
    for g in range(2):
        kcat = jnp.concatenate([k_refs[p][...].astype(bf16) for p in range(g * half, (g + 1) * half)],
                               axis=1)
        sg = lax.dot_general(kcat, qmat[:, g * half * V_HD:(g + 1) * half * V_HD], nt,
                             preferred_element_type=f32)
        s = sg if s is None else s + sg
    rowp = lax.broadcasted_iota(jnp.int32, (PAGE_ROWS, V_HD), 0)
    colp = lax.broadcasted_iota(jnp.int32, (PAGE_ROWS, V_HD), 1)
    own = jnp.bitwise_and(rowp, N_HEADS - 1) == jnp.right_shift(jnp.bitwise_and(colp, nc - 1), 1)
    s = jnp.where(own, s, NEG)
    s_new = lax.dot_general(kn.astype(bf16), qmat[:, :V_HD], nt, preferred_element_type=f32)
    own8 = (jnp.where(row8 < N_HEADS, row8, -1)
            == jnp.where(lane8 < nc, jnp.right_shift(lane8, 1), -2))
    s_new = jnp.where(own8, s_new, NEG)

    def over_pages(x, op):
        x = jnp.broadcast_to(x, (SUB, V_HD))
        for sh in (nc, 2 * nc, 4 * nc, 8 * nc):
            x = op(x, pltpu.roll(x, sh, 1))
        return x[0:1, :]

    m = over_pages(jnp.maximum(jnp.max(s, axis=0, keepdims=True),
                               jnp.max(s_new, axis=0, keepdims=True)), jnp.maximum)
    e = jnp.exp(s - m)
    e_new = jnp.exp(s_new - m)
    denom = over_pages(jnp.sum(e, axis=0, keepdims=True) + jnp.sum(e_new, axis=0, keepdims=True),
                       jnp.add)
    coef = jnp.where(jnp.bitwise_and(lane1, 1) == 0, 1.0, -lam) / denom
    return e.astype(bf16), e_new.astype(bf16), coef


def _decode_values(e, e_new, coef, vn, v_refs):
    nc = _NC
    lane1 = lax.broadcasted_iota(jnp.int32, (1, V_HD), 1)

    def lane_weights(first_lane):
        sel = jnp.where(jnp.right_shift(lane1, 3) == first_lane // nc, coef, 0.0)
        return jnp.broadcast_to(sel, (V_HD, V_HD)).astype(bf16)

    w_new = lax.dot_general(e_new, lane_weights(0), _NT, preferred_element_type=f32)
    acc = w_new * vn
    for p in range(0, N_PAGES, 2):
        both = jnp.concatenate([lane_weights(nc * p), lane_weights(nc * (p + 1))], axis=0)
        w = lax.dot_general(e, both, _NT, preferred_element_type=f32)
        wv = w[:, :V_HD] * v_refs[p][...] + w[:, V_HD:] * v_refs[p + 1][...]
        acc = acc + jnp.sum(wv.reshape(PAGE_ROWS // SUB, SUB, V_HD), axis=0)
    return acc[:N_HEADS, :] + acc[N_HEADS:, :]


def _attn_decode_call(l, page_table, q_rep, kn8, vn8, cache_k, cache_v, w_lambda, g_subln3):
    lam_init = 0.8 - 0.6 * math.exp(-0.3 * l)
    last = DEC_BATCH - 1
    cur = lambda t: jnp.minimum(t, last)
    prev = lambda t: jnp.maximum(t - 1, 0)
    row_spec = lambda tok: pl.BlockSpec((None, _NC, V_HD), lambda t, pt: (tok(t), 0, 0))

    def page_spec(tok, p):
        return pl.BlockSpec((None, None, PAGE_ROWS, V_HD), lambda t, pt: (l, pt[tok(t), p], 0, 0))

    grid_spec = pltpu.PrefetchScalarGridSpec(
        num_scalar_prefetch=1,
        grid=(DEC_BATCH + 1,),
        in_specs=[row_spec(cur), row_spec(cur), row_spec(prev),
                  pl.BlockSpec((None, 4, QK_HD), lambda t, pt: (l, 0, 0)),
                  pl.BlockSpec((None, 1, V_HD), lambda t, pt: (l, 0, 0))]
                 + [page_spec(cur, p) for p in range(N_PAGES)]
                 + [page_spec(prev, p) for p in range(N_PAGES)],
        out_specs=pl.BlockSpec((None, N_HEADS, V_HD), lambda t, pt: (prev(t), 0, 0)),
        scratch_shapes=[pltpu.VMEM((PAGE_ROWS, V_HD), bf16), pltpu.VMEM((_NC, V_HD), bf16),
                        pltpu.VMEM((1, V_HD), f32)],
    )
    return pl.pallas_call(
        functools.partial(_attn_decode_kernel, lam_init=lam_init),
        grid_spec=grid_spec,
        out_shape=jax.ShapeDtypeStruct((DEC_BATCH, N_HEADS, V_HD), f32),
        compiler_params=pltpu.CompilerParams(dimension_semantics=("arbitrary",),
                                             vmem_limit_bytes=VMEM_LIMIT),
        name=f"attn_decode_{l}",
    )(page_table, q_rep, kn8, vn8, w_lambda, g_subln3,
      *([cache_k] * N_PAGES), *([cache_v] * N_PAGES))


def _block_diag(w):
    per = LRU_BLOCKS // 2
    w = w.reshape(DEPTH, 2, per, LRU_BS, LRU_BS)
    eye = jnp.eye(per, dtype=w.dtype)
    return jnp.einsum('dgnij,nm->dgnimj', w, eye).reshape(DEPTH, 2, per * LRU_BS, per * LRU_BS)


@jax.jit
def kernel(x_prompt, x_sample, cache_k, cache_v, page_table, state_conv, state_h, p_prompt, p_sample,
           w_in, w_out, conv_w, conv_b, w_a, b_a, w_i, b_i, a_param, w_lambda, g_subln,
           w_ffn_gate, w_ffn_up, w_ffn_down, w_pe, w_pg, g_norm, g_final):
    n_dec = DEC_BATCH
    x = x_prompt.reshape(N_PROMPT, D_MODEL)
    x_dec = x_sample.reshape(n_dec, D_MODEL)
    p_pr = p_prompt.reshape(DEPTH, N_PROMPT, P_DIM)
    p_dec = p_sample.reshape(DEPTH, n_dec, P_DIM)

    wg_b = w_ffn_gate.astype(bf16)
    wu_b = w_ffn_up.astype(bf16)
    wd_b = w_ffn_down.astype(bf16)
    win_b = w_in.astype(bf16)
    wo_b = w_out.astype(bf16)
    wpg_b = w_pg.astype(bf16)
    wpe_b = w_pe.astype(bf16)
    wa_b = _block_diag(w_a).astype(bf16)
    wi_b = _block_diag(w_i).astype(bf16)
    g_norm3 = g_norm.reshape(DEPTH, 4, 1, D_MODEL)
    g_final2 = g_final.reshape(1, D_MODEL)
    g_subln3 = g_subln.reshape(DEPTH, 1, V_HD)
    cb3 = conv_b.reshape(DEPTH, 1, LRU_W)
    ba3 = b_a.reshape(DEPTH, 1, LRU_W)
    bi3 = b_i.reshape(DEPTH, 1, LRU_W)
    ap3 = a_param.reshape(DEPTH, 1, LRU_W)
    sc_t = state_conv.transpose(0, 2, 1, 3)
    ck = cache_k.reshape(DEPTH, -1, PAGE_ROWS, V_HD)
    cv = cache_v.reshape(DEPTH, -1, PAGE_ROWS, V_HD)
    gsb = jnp.broadcast_to(g_subln[:, :, None], (DEPTH, V_HD, TQ))
    tabs = _rope_tables()

    ks_s, vs_s, convs_p, hs_p, convs_s, hs_s = [], [], [], [], [], []
    kfin = vfin = None
    dec_rows = n_dec * N_HEADS
    for l in range(DEPTH):
        x, u, gate, qs, qt, kb, vt, kfin, vfin, ksn, vsn = _x_call(
            l, x, x_dec, g_norm3, wg_b, wu_b, wd_b, win_b, tabs, kfin, vfin)
        k_new = ksn[:dec_rows].reshape(n_dec, N_HEADS, V_HD)
        v_new = vsn[:dec_rows].reshape(n_dec, N_HEADS, V_HD)
        ks_s.append(k_new)
        vs_s.append(v_new)

        o_lru, conv_p, h_p = _lru_prompt_call(l, u, gate, conv_w, cb3, wa_b, wi_b, ba3, bi3, ap3)
        o_lru_s, conv_s, h_s = _lru_decode_call(l, u, gate, sc_t, state_h,
                                                conv_w, cb3, wa_b, wi_b, ba3, bi3, ap3)
        convs_p.append(conv_p)
        hs_p.append(h_p.reshape(BATCH, LRU_W))
        convs_s.append(conv_s.transpose(1, 0, 2))
        hs_s.append(h_s)

        o_attn_t = _attn_prompt_call(l, qt, kb, vt, w_lambda, gsb)
        q_rep = jnp.repeat(qs[:n_dec].reshape(n_dec, N_HEADS, V_HD), 2, axis=1)
        pad8 = ((0, 0), (0, 2 * N_HEADS - N_HEADS), (0, 0))
        o_attn_s = _attn_decode_call(l, page_table, q_rep, jnp.pad(k_new, pad8), jnp.pad(v_new, pad8),
                                     ck, cv, w_lambda, g_subln3)
        o_attn_t_s = o_attn_s.reshape(n_dec, ATT_W).astype(bf16).T

        x = _y_call(l, x, o_lru, o_lru_s, o_attn_t, o_attn_t_s, p_pr, p_dec, wo_b, g_norm3,
                    wg_b, wu_b, wd_b, wpg_b, wpe_b, g_final2)
    y_p, y_s = x
    y_prompt = y_p.reshape(BATCH, SEQ, D_MODEL)
    y_sample = y_s[:n_dec].reshape(n_dec, 1, D_MODEL)
    k_prompt = kfin.reshape(DEPTH, BATCH, SEQ, N_HEADS, V_HD)
    v_prompt = vfin.reshape(DEPTH, BATCH, SEQ, N_HEADS, V_HD)
    k_sample = jnp.stack(ks_s).reshape(DEPTH, n_dec, 1, N_HEADS, V_HD)
    v_sample = jnp.stack(vs_s).reshape(DEPTH, n_dec, 1, N_HEADS, V_HD)
    return (y_prompt, y_sample, k_prompt, v_prompt, jnp.stack(convs_p), jnp.stack(hs_p),
            k_sample, v_sample, jnp.stack(convs_s), jnp.stack(hs_s))
```
